```python
import math
import jax, jax.numpy as jnp
from jax import lax
import numpy as np

D_MODEL = 1024
BATCH = 16
SEQ = 2048
DEPTH = 1

CTX_LEN = 256
GRID_W = 64
DN_HEADS = 8
DN_HEAD_DIM = 128
DN_WIDTH = DN_HEADS * DN_HEAD_DIM
DN_CONV = 4
CHUNK = 64
LRU_WIDTH = 1024
LRU_BLOCKS = 16
LRU_BLOCK_DIM = LRU_WIDTH // LRU_BLOCKS
LRU_CONV = 4
LRU_C = 8.0
N_DIRS = 2
N_BRANCH = 2
D_FF = 4 * D_MODEL
FFN_CONV = 3
EPS = 1e-6
IN_SIZES = (DN_WIDTH, DN_WIDTH, DN_WIDTH, DN_WIDTH, N_DIRS * DN_HEADS, N_DIRS * DN_HEADS,
            LRU_WIDTH, LRU_WIDTH, N_BRANCH * D_MODEL)
D_IN = sum(IN_SIZES)

kernel_name = "hybrid_deltanet_rglru_convffn_dit"


def rmsnorm(u, gain):
    u32 = u.astype(jnp.float32)
    y = u32 * lax.rsqrt(jnp.mean(u32 * u32, axis=-1, keepdims=True) + EPS)
    return (y * gain.astype(jnp.float32)).astype(u.dtype)


def l2norm(u):
    return u * lax.rsqrt(jnp.sum(u * u, axis=-1, keepdims=True) + EPS)


def modulate(u, shift_c, scale_c, shift_x, scale_x, n_ctx):
    return jnp.concatenate([u[:, :n_ctx] * (1 + scale_c) + shift_c,
                            u[:, n_ctx:] * (1 + scale_x[:, None]) + shift_x[:, None]], axis=1)


def gate_rows(v, gate_c, gate_x, n_ctx):
    return jnp.concatenate([v[:, :n_ctx] * gate_c, v[:, n_ctx:] * gate_x[:, None]], axis=1)


def dwconv1d(u, w, pad_left):
    K, C = w.shape
    return lax.conv_general_dilated(u, w[:, None, :].astype(u.dtype), (1,), [(pad_left, K - 1 - pad_left)],
                                    dimension_numbers=('NWC', 'WIO', 'NWC'), feature_group_count=C)


def seg_dwconv1d(u, w, n_ctx, pad_left):
    return jnp.concatenate([dwconv1d(u[:, :n_ctx], w, pad_left), dwconv1d(u[:, n_ctx:], w, pad_left)], axis=1)


def ffn_dwconv(u, w, n_ctx, rows):
    B, T, C = u.shape
    lat = lax.conv_general_dilated(u[:, n_ctx:].reshape(B, rows, GRID_W, C), w[:, :, None, :].astype(u.dtype),
                                   (1, 1), 'SAME', dimension_numbers=('NHWC', 'HWIO', 'NHWC'),
                                   feature_group_count=C).reshape(B, T - n_ctx, C)
    if n_ctx == 0:
        return lat
    ctx_part = dwconv1d(u[:, :n_ctx], w[FFN_CONV // 2], FFN_CONV // 2)
    return jnp.concatenate([ctx_part, lat], axis=1)


def seg_flip(u, n_ctx):
    return jnp.concatenate([jnp.flip(u[:, :n_ctx], axis=1), jnp.flip(u[:, n_ctx:], axis=1)], axis=1)


def to_dirs(u_fwd, u_bwd, n_ctx):
    return jnp.stack([u_fwd, seg_flip(u_bwd, n_ctx)])


def from_dirs(o, n_ctx):
    return o[0] + seg_flip(o[1], n_ctx)


def delta_rule_chunked(q, k, v, g, beta):
    *lead, T, dk = q.shape
    dv = v.shape[-1]
    n = T // CHUNK
    q = q.reshape(*lead, n, CHUNK, dk)
    k = k.reshape(*lead, n, CHUNK, dk)
    v = v.reshape(*lead, n, CHUNK, dv)
    beta = beta.reshape(*lead, n, CHUNK)
    g = jnp.cumsum(g.reshape(*lead, n, CHUNK), axis=-1)
    incl = jnp.tril(jnp.ones((CHUNK, CHUNK), bool))
    strict = jnp.tril(jnp.ones((CHUNK, CHUNK), bool), -1)
    decay = jnp.exp(jnp.where(incl, g[..., :, None] - g[..., None, :], -jnp.inf))
    k_beta = k * beta[..., None]
    lower = jnp.where(strict, jnp.einsum('...ik,...jk->...ij', k_beta, k) * decay, 0.0)
    eye = jnp.eye(CHUNK, dtype=q.dtype)
    t_inv = lax.linalg.triangular_solve(eye + lower, jnp.broadcast_to(eye, lower.shape),
                                        left_side=True, lower=True, unit_diagonal=True)
    w = t_inv @ (k_beta * jnp.exp(g)[..., None])
    u = t_inv @ (v * beta[..., None])
    attn = jnp.where(incl, jnp.einsum('...ik,...jk->...ij', q, k) * decay, 0.0)

    def step(state, xs):
        qc, kc, uc, wc, ac, gc = xs
        v_new = uc - wc @ state
        out = (qc * jnp.exp(gc)[..., None]) @ state + ac @ v_new
        g_last = gc[..., -1:]
        state = state * jnp.exp(g_last)[..., None] + jnp.einsum(
            '...ck,...cv->...kv', kc * jnp.exp(g_last - gc)[..., None], v_new)
        return state, out

    xs = tuple(jnp.moveaxis(t, len(lead), 0) for t in (q, k, u, w, attn, g))
    state0 = jnp.zeros((*lead, dk, dv), q.dtype)
    _, out = lax.scan(step, state0, xs)
    return jnp.moveaxis(out, 0, len(lead)).reshape(*lead, T, dv)


def gated_deltanet(q, k, v, z, a, b, w_conv, a_log, dt_bias, onorm, n_ctx):
    B, T, _ = q.shape
    out_dtype = q.dtype
    f32 = jnp.float32
    qkv = jax.nn.silu(seg_dwconv1d(jnp.concatenate([q, k, v], axis=-1), w_conv, n_ctx, DN_CONV // 2))
    q, k, v = jnp.split(qkv.astype(f32), 3, axis=-1)
    heads = lambda t: t.reshape(B, T, DN_HEADS, DN_HEAD_DIM)
    q = l2norm(heads(q)) * DN_HEAD_DIM ** -0.5
    k = l2norm(heads(k))
    v = heads(v)
    g = -jnp.exp(a_log.astype(f32)) * jax.nn.softplus(a.astype(f32) + dt_bias.astype(f32))
    beta = jax.nn.sigmoid(b.astype(f32))
    tth = lambda t: jnp.moveaxis(t, 2, 3)
    o = delta_rule_chunked(tth(to_dirs(q, q, n_ctx)), tth(to_dirs(k, k, n_ctx)), tth(to_dirs(v, v, n_ctx)),
                           tth(to_dirs(g[:, :, 0], g[:, :, 1], n_ctx)),
                           tth(to_dirs(beta[:, :, 0], beta[:, :, 1], n_ctx)))
    o = from_dirs(jnp.moveaxis(o, 3, 2), n_ctx)
    o = rmsnorm(o, onorm) * jax.nn.silu(heads(z).astype(f32))
    return o.reshape(B, T, DN_WIDTH).astype(out_dtype)


def rg_lru(xb, yb, w_conv, b_conv, w_rg, b_rg, w_ig, b_ig, lam, n_ctx):
    B, T, W = xb.shape
    out_dtype = xb.dtype
    f32 = jnp.float32
    xc = (seg_dwconv1d(xb, w_conv, n_ctx, LRU_CONV // 2) + b_conv).astype(f32)
    xd = to_dirs(xc, xc, n_ctx)
    blocks = xd.reshape(N_DIRS, B, T, LRU_BLOCKS, LRU_BLOCK_DIM)

    def gate(w, bias):
        y = jnp.einsum('dbthi,dhij->dbthj', blocks, w.astype(f32)).reshape(N_DIRS, B, T, W)
        return jax.nn.sigmoid(y + bias.astype(f32)[:, None, None])

    r = gate(w_rg, b_rg)
    i = gate(w_ig, b_ig)
    log_a = -LRU_C * r * jax.nn.softplus(-lam.astype(f32))[:, None, None]
    a = jnp.exp(log_a)
    inp = jnp.sqrt(-jnp.expm1(2.0 * log_a)) * (i * xd)
    _, h = lax.associative_scan(lambda e, l: (e[0] * l[0], l[0] * e[1] + l[1]), (a, inp), axis=2)
    h = from_dirs(h, n_ctx)
    return (h * jax.nn.gelu(yb.astype(f32))).astype(out_dtype)


def setup_inputs(seed: int = 0) -> dict:
    key = jax.random.key(seed)
    ks = jax.random.split(key, 32)
    f32 = jnp.float32
    L = DEPTH
    nrm = lambda k, shape, scale: jax.random.normal(k, shape, f32) * scale
    x = nrm(ks[0], (BATCH, SEQ, D_MODEL), 1.0)
    c = nrm(ks[1], (BATCH, D_MODEL), 1.0)
    ctx = nrm(ks[2], (BATCH, CTX_LEN, D_MODEL), 1.0)
    c_ctx = nrm(ks[3], (D_MODEL,), 1.0)
    w_ada = nrm(ks[4], (L, D_MODEL, 6 * D_MODEL), 0.5 * D_MODEL ** -0.5)
    b_ada = nrm(ks[5], (L, 6 * D_MODEL), 0.02)
    g_pre_mix = 1.0 + nrm(ks[6], (L, D_MODEL), 0.05)
    g_post_mix = 1.0 + nrm(ks[7], (L, D_MODEL), 0.05)
    g_pre_ffn = 1.0 + nrm(ks[8], (L, D_MODEL), 0.05)
    g_post_ffn = 1.0 + nrm(ks[9], (L, D_MODEL), 0.05)
    w_in = nrm(ks[10], (L, D_MODEL, D_IN), D_MODEL ** -0.5)
    b_merge = nrm(ks[11], (L, N_BRANCH * D_MODEL), 0.02)
    dn_conv = nrm(ks[12], (L, DN_CONV, 3 * DN_WIDTH), DN_CONV ** -0.5)
    dn_a_log = jnp.log(jax.random.uniform(ks[13], (L, N_DIRS, DN_HEADS), f32, 1.0, 16.0))
    dt = jnp.exp(jax.random.uniform(ks[14], (L, N_DIRS, DN_HEADS), f32, math.log(1e-3), math.log(1e-1)))
    dn_dt_bias = dt + jnp.log(-jnp.expm1(-dt))
    dn_onorm = 1.0 + nrm(ks[15], (L, DN_HEAD_DIM), 0.05)
    lru_conv = nrm(ks[16], (L, LRU_CONV, LRU_WIDTH), LRU_CONV ** -0.5)
    lru_conv_b = nrm(ks[17], (L, LRU_WIDTH), 0.02)
    lru_w_rg = nrm(ks[18], (L, N_DIRS, LRU_BLOCKS, LRU_BLOCK_DIM, LRU_BLOCK_DIM), LRU_BLOCK_DIM ** -0.5)
    lru_b_rg = nrm(ks[19], (L, N_DIRS, LRU_WIDTH), 0.02)
    lru_w_ig = nrm(ks[20], (L, N_DIRS, LRU_BLOCKS, LRU_BLOCK_DIM, LRU_BLOCK_DIM), LRU_BLOCK_DIM ** -0.5)
    lru_b_ig = nrm(ks[21], (L, N_DIRS, LRU_WIDTH), 0.02)
    a0 = jax.random.uniform(ks[22], (L, N_DIRS, LRU_WIDTH), f32, 0.9, 0.999)
    s = a0 ** (1.0 / LRU_C)
    lru_lambda = jnp.log(s) - jnp.log1p(-s)
    w_branch_dn = nrm(ks[23], (L, DN_WIDTH, D_MODEL), DN_WIDTH ** -0.5)
    w_branch_lru = nrm(ks[24], (L, LRU_WIDTH, D_MODEL), LRU_WIDTH ** -0.5)
    w_out = nrm(ks[25], (L, D_MODEL, D_MODEL), D_MODEL ** -0.5)
    w_up = nrm(ks[26], (L, D_MODEL, 2 * D_FF), D_MODEL ** -0.5)
    ffn_dw = nrm(ks[27], (L, FFN_CONV, FFN_CONV, D_FF), 1.0 / FFN_CONV)
    ffn_dw_b = nrm(ks[28], (L, D_FF), 0.02)
    w_down = nrm(ks[29], (L, D_FF, D_MODEL), D_FF ** -0.5)
    return {"x": x, "c": c, "ctx": ctx, "c_ctx": c_ctx, "w_ada": w_ada, "b_ada": b_ada,
            "g_pre_mix": g_pre_mix, "g_post_mix": g_post_mix, "g_pre_ffn": g_pre_ffn, "g_post_ffn": g_post_ffn,
            "w_in": w_in, "b_merge": b_merge, "dn_conv": dn_conv, "dn_a_log": dn_a_log,
            "dn_dt_bias": dn_dt_bias, "dn_onorm": dn_onorm, "lru_conv": lru_conv, "lru_conv_b": lru_conv_b,
            "lru_w_rg": lru_w_rg, "lru_b_rg": lru_b_rg, "lru_w_ig": lru_w_ig, "lru_b_ig": lru_b_ig,
            "lru_lambda": lru_lambda, "w_branch_dn": w_branch_dn, "w_branch_lru": w_branch_lru,
            "w_out": w_out, "w_up": w_up, "ffn_dw": ffn_dw, "ffn_dw_b": ffn_dw_b, "w_down": w_down}


def reference(x, c, ctx, c_ctx, w_ada, b_ada, g_pre_mix, g_post_mix, g_pre_ffn, g_post_ffn, w_in, b_merge,
              dn_conv, dn_a_log, dn_dt_bias, dn_onorm, lru_conv, lru_conv_b, lru_w_rg, lru_b_rg, lru_w_ig,
              lru_b_ig, lru_lambda, w_branch_dn, w_branch_lru, w_out, w_up, ffn_dw, ffn_dw_b, w_down):
    n_ctx = ctx.shape[1]
    rows = x.shape[1] // GRID_W
    B = x.shape[0]
    split_at = [int(s) for s in np.cumsum(IN_SIZES)[:-1]]
    h = jnp.concatenate([ctx, x], axis=1)
    silu_c = jax.nn.silu(c)
    silu_cc = jax.nn.silu(c_ctx)
    for l in range(DEPTH):
        last = l == DEPTH - 1
        m_x = jnp.split(silu_c @ w_ada[l] + b_ada[l], 6, axis=-1)
        m_c = jnp.split(silu_cc @ w_ada[l] + b_ada[l], 6, axis=-1)
        u = modulate(rmsnorm(h, g_pre_mix[l]), m_c[0], m_c[1], m_x[0], m_x[1], n_ctx)
        T = u.shape[1]
        q, k, v, z, a, b, xl, yl, mg = jnp.split(u @ w_in[l], split_at, axis=-1)
        y_dn = gated_deltanet(q, k, v, z, a.reshape(B, T, N_DIRS, DN_HEADS), b.reshape(B, T, N_DIRS, DN_HEADS),
                              dn_conv[l], dn_a_log[l], dn_dt_bias[l], dn_onorm[l], n_ctx)
        y_lru = rg_lru(xl, yl, lru_conv[l], lru_conv_b[l], lru_w_rg[l], lru_b_rg[l], lru_w_ig[l], lru_b_ig[l],
                       lru_lambda[l], n_ctx)
        drop = n_ctx if last else 0
        n_cur = n_ctx - drop
        h, y_dn, y_lru, mg = h[:, drop:], y_dn[:, drop:], y_lru[:, drop:], mg[:, drop:]
        g_dn, g_lru = jnp.split(jax.nn.sigmoid(mg + b_merge[l]), N_BRANCH, axis=-1)
        mix = (g_dn * (y_dn @ w_branch_dn[l]) + g_lru * (y_lru @ w_branch_lru[l])) @ w_out[l]
        h = h + gate_rows(rmsnorm(mix, g_post_mix[l]), m_c[2], m_x[2], n_cur)
        u = modulate(rmsnorm(h, g_pre_ffn[l]), m_c[3], m_c[4], m_x[3], m_x[4], n_cur)
        f_gate, f_val = jnp.split(u @ w_up[l], 2, axis=-1)
        f_gate = ffn_dwconv(f_gate, ffn_dw[l], n_cur, rows) + ffn_dw_b[l]
        f = jax.nn.gelu(f_gate) * f_val
        h = h + gate_rows(rmsnorm(f @ w_down[l], g_post_ffn[l]), m_c[5], m_x[5], n_cur)
    return h
```

```python
import functools

import jax
import jax.numpy as jnp
from jax import lax
from jax.experimental import pallas as pl
from jax.experimental.pallas import tpu as pltpu

F32 = jnp.float32
BF16 = jnp.bfloat16

EPS = 1e-6
D_MODEL = 1024
GRID_W = 64
DN_HEADS = 8
DN_HEAD_DIM = 128
DN_WIDTH = DN_HEADS * DN_HEAD_DIM
CHUNK = 64
LRU_WIDTH = 1024
LRU_BLOCKS = 16
LRU_BLOCK_DIM = LRU_WIDTH // LRU_BLOCKS
LRU_C = 8.0
N_DIRS = 2
D_FF = 4 * D_MODEL
N_AB = 2 * N_DIRS * DN_HEADS
D_MAIN = 4 * DN_WIDTH + 2 * LRU_WIDTH + 2 * D_MODEL

LANES = 128
SUBLANES = 8
LRU_GROUP = 256
N_LRU_GROUPS = LRU_WIDTH // LRU_GROUP
VMEM_LIMIT = 56 * 1024 * 1024


def _bdot(a, b):
    return jnp.dot(a.astype(BF16), b.astype(BF16), preferred_element_type=F32)


def _split(a):
    hi = a.astype(BF16)
    return hi, (a - hi.astype(F32)).astype(BF16)


def _dot3(a, b):
    a_hi, a_lo = _split(a)
    b_hi, b_lo = _split(b)
    d = functools.partial(jnp.dot, preferred_element_type=F32)
    return d(a_hi, b_hi) + (d(a_hi, b_lo) + d(a_lo, b_hi))


def _silu(x):
    return x * jax.nn.sigmoid(x)


def _softplus(x):
    return jnp.maximum(x, 0.0) + jnp.log1p(jnp.exp(-jnp.abs(x)))


def _gelu_tanh(x):
    return 0.5 * x * (1.0 + jnp.tanh(0.7978845608028654 * (x + 0.044715 * (x * x * x))))


def _rmsnorm(v, gain):
    ms = jnp.mean(v * v, axis=-1, keepdims=True)
    return v * lax.rsqrt(ms + EPS) * gain


def _seg_pos(t_all, n_ctx):
    row = lax.broadcasted_iota(jnp.int32, (t_all, 1), 0)
    in_ctx = row < n_ctx
    return jnp.where(in_ctx, row, row - n_ctx), jnp.where(in_ctx, n_ctx, t_all - n_ctx)


def _seg_conv4(x, w, pos, seglen):
    t_all = x.shape[0]
    acc = x * w[2:3]
    acc += jnp.where(pos >= 2, pltpu.roll(x, 2, 0), 0.0) * w[0:1]
    acc += jnp.where(pos >= 1, pltpu.roll(x, 1, 0), 0.0) * w[1:2]
    acc += jnp.where(pos <= seglen - 2, pltpu.roll(x, t_all - 1, 0), 0.0) * w[3:4]
    return acc


def _ada_kernel(c_ref, w_ref, b_ref, o_ref):
    o_ref[...] = _dot3(_silu(c_ref[...]), w_ref[...]) + b_ref[...]


def _ada(cc, w_ada, b_ada):
    rows, d = cc.shape
    n = w_ada.shape[1]
    tn = 1536
    return pl.pallas_call(
        _ada_kernel,
        grid=(n // tn,),
        in_specs=[pl.BlockSpec((rows, d), lambda j: (0, 0)),
                  pl.BlockSpec((d, tn), lambda j: (0, j)),
                  pl.BlockSpec((1, tn), lambda j: (0, j))],
        out_specs=pl.BlockSpec((rows, tn), lambda j: (0, j)),
        out_shape=jax.ShapeDtypeStruct((rows, n), F32),
        compiler_params=pltpu.CompilerParams(dimension_semantics=("arbitrary",),
                                             vmem_limit_bytes=VMEM_LIMIT),
        name="ada",
    )(cc, w_ada, b_ada)


def _chunk_cumsum(g, reverse):
    t_all = g.shape[0]
    pos = lax.broadcasted_iota(jnp.int32, (t_all, 1), 0) % CHUNK
    s = 1
    while s < CHUNK:
        if reverse:
            g = g + jnp.where(pos < CHUNK - s, pltpu.roll(g, t_all - s, 0), 0.0)
        else:
            g = g + jnp.where(pos >= s, pltpu.roll(g, s, 0), 0.0)
        s *= 2
    return g


def _inproj_kernel(n_ctx, ctx_ref, x_ref, shc_ref, scc_ref, shx_ref, scx_ref, g_ref, w_ref, wab_ref,
                   alog_ref, dtb_ref, o_ref, gates_ref, u_ref):
    @pl.when(pl.program_id(1) == 0)
    def _():
        gain = g_ref[...]
        u_ref[0:n_ctx] = (_rmsnorm(ctx_ref[0], gain) * (1.0 + scc_ref[0]) + shc_ref[0]).astype(BF16)
        u_ref[n_ctx:] = (_rmsnorm(x_ref[0], gain) * (1.0 + scx_ref[0]) + shx_ref[0]).astype(BF16)
        ab = jnp.dot(u_ref[...], wab_ref[...], preferred_element_type=F32)
        g = -jnp.exp(alog_ref[...]) * _softplus(ab + dtb_ref[...])
        lane = lax.broadcasted_iota(jnp.int32, (1, LANES), 1)
        gc = jnp.where(lane < DN_HEADS, _chunk_cumsum(g, False), _chunk_cumsum(g, True))
        gates_ref[0] = jnp.where(lane < N_DIRS * DN_HEADS, gc, jax.nn.sigmoid(ab))

    o_ref[0] = jnp.dot(u_ref[...], w_ref[...], preferred_element_type=F32).astype(BF16)


def _inproj(ctx, x, mod3, g_pre, w_main, w_ab, alog, dtb):
    b_sz, n_ctx, d = ctx.shape
    n_lat = x.shape[1]
    t_all = n_ctx + n_lat
    tn = 512
    c_row = b_sz
    vec = lambda k, ctx_row: pl.BlockSpec(
        (1, 1, d), (lambda b, j: (c_row, 0, k)) if ctx_row else (lambda b, j: (b, 0, k)))
    return pl.pallas_call(
        functools.partial(_inproj_kernel, n_ctx),
        grid=(b_sz, D_MAIN // tn),
        in_specs=[pl.BlockSpec((1, n_ctx, d), lambda b, j: (b, 0, 0)),
                  pl.BlockSpec((1, n_lat, d), lambda b, j: (b, 0, 0)),
                  vec(0, True), vec(1, True), vec(0, False), vec(1, False),
                  pl.BlockSpec((1, d), lambda b, j: (0, 0)),
                  pl.BlockSpec((d, tn), lambda b, j: (0, j)),
                  pl.BlockSpec((d, LANES), lambda b, j: (0, 0)),
                  pl.BlockSpec((1, LANES), lambda b, j: (0, 0)),
                  pl.BlockSpec((1, LANES), lambda b, j: (0, 0))],
        out_specs=[pl.BlockSpec((1, t_all, tn), lambda b, j: (b, 0, j)),
                   pl.BlockSpec((1, t_all, LANES), lambda b, j: (b, 0, 0))],
        out_shape=[jax.ShapeDtypeStruct((b_sz, t_all, D_MAIN), BF16),
                   jax.ShapeDtypeStruct((b_sz, t_all, LANES), F32)],
        scratch_shapes=[pltpu.VMEM((t_all, d), BF16)],
        compiler_params=pltpu.CompilerParams(dimension_semantics=("arbitrary", "arbitrary"),
                                             vmem_limit_bytes=VMEM_LIMIT),
        name="inproj",
    )(ctx, x, mod3, mod3, mod3, mod3, g_pre, w_main, w_ab, alog, dtb)


def _unit_tri_inverse(low):
    eye = (lax.broadcasted_iota(jnp.int32, low.shape, 0)
           == lax.broadcasted_iota(jnp.int32, low.shape, 1)).astype(F32)
    inv = eye - low
    power = low
    n = 2
    while n < CHUNK:
        power = _bdot(power, power)
        inv = inv + _bdot(inv, power)
        n *= 2
    return inv


def _dn_chunk_terms(d, q, k, v, gc, beta, gr):
    ii = lax.broadcasted_iota(jnp.int32, (CHUNK, CHUNK), 0)
    jj = lax.broadcasted_iota(jnp.int32, (CHUNK, CHUNK), 1)
    incl = (ii >= jj) if d == 0 else (ii <= jj)
    strict = (ii > jj) if d == 0 else (ii < jj)
    decay = jnp.where(incl, jnp.exp(jnp.minimum(gc - gr, 0.0)), 0.0)
    kb = k * beta
    eg = jnp.exp(gc)
    g_last = gc[CHUNK - 1:CHUNK] if d == 0 else gc[0:1]
    kq = jnp.concatenate([kb, q], axis=0).astype(BF16)
    aq = lax.dot_general(kq, k.astype(BF16), (((1,), (1,)), ((), ())), preferred_element_type=F32)
    low = jnp.where(strict, aq[:CHUNK] * decay, 0.0)
    attn = aq[CHUNK:] * decay
    t_inv = _unit_tri_inverse(low)
    wu = _bdot(t_inv, jnp.concatenate([kb * eg, v * beta], axis=1)).astype(BF16)
    kg = (k * jnp.exp(g_last - gc)).astype(BF16)
    au = jnp.dot(attn.astype(BF16), wu, preferred_element_type=F32)
    ku = lax.dot_general(kg, wu, (((0,), (0,)), ((), ())), preferred_element_type=F32)
    dk = q.shape[1]
    return -ku[:, :dk], q * eg - au[:, :dk], ku[:, dk:], au[:, dk:]


def _dn_kernel(n_ctx, q_ref, k_ref, v_ref, z_ref, gcol_ref, grow_ref, wq_ref, wk_ref, wv_ref, on_ref,
               o_ref, q_s, k_s, v_s, mq_s, add_s, o0_s, out_s):
    t_all = q_ref.shape[1]
    n_chunks = t_all // CHUNK
    ctx_chunks = n_ctx // CHUNK
    dk = DN_HEAD_DIM
    pos, seglen = _seg_pos(t_all, n_ctx)

    def conv_silu(ref, w_ref):
        return _silu(_seg_conv4(ref[0].astype(F32), w_ref[...], pos, seglen))

    def l2n(t):
        return t * lax.rsqrt(jnp.sum(t * t, axis=-1, keepdims=True) + EPS)

    q_s[...] = l2n(conv_silu(q_ref, wq_ref)) * (DN_HEAD_DIM ** -0.5)
    k_s[...] = l2n(conv_silu(k_ref, wk_ref))
    v_s[...] = conv_silu(v_ref, wv_ref)

    def pre(c, carry):
        r0 = pl.multiple_of(c * CHUNK, CHUNK)
        rows = pl.ds(r0, CHUNK)
        q, k, v = q_s[rows, :], k_s[rows, :], v_s[rows, :]
        for d in range(N_DIRS):
            gc = gcol_ref[0, 0, rows, d:d + 1]
            beta = gcol_ref[0, 0, rows, N_DIRS + d:N_DIRS + d + 1]
            gr = grow_ref[0, 0, d, pl.ds(c, 1), :]
            trans, qe, add, o0 = _dn_chunk_terms(d, q, k, v, gc, beta, gr)
            mq_s[d, c, 0:dk, :] = trans.astype(BF16)
            mq_s[d, c, dk:, :] = qe.astype(BF16)
            add_s[d, c] = add
            o0_s[d, rows, :] = o0
        return carry

    lax.fori_loop(0, n_chunks, pre, 0, unroll=2)

    def seq(s, states):
        chunk_of = (s, jnp.where(s < ctx_chunks, ctx_chunks - 1 - s, n_chunks + ctx_chunks - 1 - s))
        new_states = []
        for d in range(N_DIRS):
            c = chunk_of[d]
            rows = pl.ds(pl.multiple_of(c * CHUNK, CHUNK), CHUNK)
            edge = c * CHUNK + (CHUNK - 1 if d == 0 else 0)
            g_last = gcol_ref[0, 0, pl.ds(edge, 1), d:d + 1]
            r = jnp.dot(mq_s[d, c], states[d].astype(BF16), preferred_element_type=F32)
            out_s[d, rows, :] = r[dk:] + o0_s[d, rows, :]
            new_states.append(states[d] * jnp.exp(g_last) + (r[:dk] + add_s[d, c]))
        return tuple(new_states)

    zero = jnp.zeros((dk, dk), F32)
    lax.fori_loop(0, n_chunks, seq, (zero, zero))

    o = out_s[0, n_ctx:, :] + out_s[1, n_ctx:, :]
    y = _rmsnorm(o, on_ref[...]) * _silu(z_ref[0, n_ctx:, :].astype(F32))
    o_ref[0] = y.astype(BF16)


def _deltanet(proj, gcol, grow, dn_conv, onorm, n_ctx):
    b_sz, t_all, _ = proj.shape
    n_lat = t_all - n_ctx
    n_chunks = t_all // CHUNK
    hd = DN_HEAD_DIM
    col = lambda part: pl.BlockSpec((1, t_all, hd), lambda b, h: (b, 0, part * DN_HEADS + h))
    wcol = lambda part: pl.BlockSpec((dn_conv.shape[0], hd), lambda b, h: (0, part * DN_HEADS + h))
    return pl.pallas_call(
        functools.partial(_dn_kernel, n_ctx),
        grid=(b_sz, DN_HEADS),
        in_specs=[col(0), col(1), col(2), col(3),
                  pl.BlockSpec((1, 1, t_all, 2 * N_DIRS), lambda b, h: (b, h, 0, 0)),
                  pl.BlockSpec((1, 1, N_DIRS, n_chunks, CHUNK), lambda b, h: (b, h, 0, 0, 0)),
                  wcol(0), wcol(1), wcol(2),
                  pl.BlockSpec((1, hd), lambda b, h: (0, 0))],
        out_specs=pl.BlockSpec((1, n_lat, hd), lambda b, h: (b, 0, h)),
        out_shape=jax.ShapeDtypeStruct((b_sz, n_lat, DN_WIDTH), BF16),
        scratch_shapes=[pltpu.VMEM((t_all, hd), F32), pltpu.VMEM((t_all, hd), F32),
                        pltpu.VMEM((t_all, hd), F32),
                        pltpu.VMEM((N_DIRS, n_chunks, hd + CHUNK, hd), BF16),
                        pltpu.VMEM((N_DIRS, n_chunks, hd, hd), F32),
                        pltpu.VMEM((N_DIRS, t_all, hd), F32),
                        pltpu.VMEM((N_DIRS, t_all, hd), F32)],
        compiler_params=pltpu.CompilerParams(dimension_semantics=("arbitrary", "arbitrary"),
                                             vmem_limit_bytes=VMEM_LIMIT),
        name="deltanet",
    )(proj, proj, proj, proj, gcol, grow, dn_conv, dn_conv, dn_conv, onorm)


LRU_ROW_TILE = 256


def _lru_kernel(n_ctx, xl_ref, yl_ref, wc_ref, bc_ref, wg_ref, bg_ref, lam_ref, o_ref,
                xc_s, a_s, b_s, h_s):
    t_all = xl_ref.shape[1]
    gw = LRU_GROUP
    pos, seglen = _seg_pos(t_all, n_ctx)
    xc_s[...] = _seg_conv4(xl_ref[0].astype(F32), wc_ref[...], pos, seglen) + bc_ref[...]
    sp = _softplus(-lam_ref[0])
    sub = lax.broadcasted_iota(jnp.int32, (LRU_ROW_TILE, 1), 0) % SUBLANES

    def gates(i, carry):
        rows = pl.ds(pl.multiple_of(i * LRU_ROW_TILE, LRU_ROW_TILE), LRU_ROW_TILE)
        xc = xc_s[rows, :]
        y = jnp.dot(xc.astype(BF16), wg_ref[0], preferred_element_type=F32) + bg_ref[0]
        for d in range(N_DIRS):
            r = jax.nn.sigmoid(y[:, (2 * d) * gw:(2 * d + 1) * gw])
            ig = jax.nn.sigmoid(y[:, (2 * d + 1) * gw:(2 * d + 2) * gw])
            log_a = (-LRU_C) * r * sp[:, d * gw:(d + 1) * gw]
            a = jnp.exp(log_a)
            b = jnp.sqrt(-jnp.tanh(log_a) * (a * a + 1.0)) * (ig * xc)
            s = 1
            while s < SUBLANES:
                if d == 0:
                    ok = sub >= s
                    a_sh, b_sh = pltpu.roll(a, s, 0), pltpu.roll(b, s, 0)
                else:
                    ok = sub < SUBLANES - s
                    a_sh = pltpu.roll(a, LRU_ROW_TILE - s, 0)
                    b_sh = pltpu.roll(b, LRU_ROW_TILE - s, 0)
                b = b + a * jnp.where(ok, b_sh, 0.0)
                a = a * jnp.where(ok, a_sh, 1.0)
                s *= 2
            a_s[d, rows, :] = a
            b_s[d, rows, :] = b
        return carry

    lax.fori_loop(0, t_all // LRU_ROW_TILE, gates, 0)

    n_tiles = t_all // SUBLANES
    ctx_tiles = n_ctx // SUBLANES

    def carry_step(s, carries):
        tile_of = (s, jnp.where(s < ctx_tiles, ctx_tiles - 1 - s, n_tiles + ctx_tiles - 1 - s))
        new = []
        for d in range(N_DIRS):
            rows = pl.ds(pl.multiple_of(tile_of[d] * SUBLANES, SUBLANES), SUBLANES)
            h = b_s[d, rows, :] + a_s[d, rows, :] * carries[d]
            h_s[d, rows, :] = h
            edge = h[SUBLANES - 1:SUBLANES] if d == 0 else h[0:1]
            new.append(jnp.broadcast_to(edge, h.shape))
        return tuple(new)

    zero = jnp.zeros((SUBLANES, gw), F32)
    lax.fori_loop(0, n_tiles, carry_step, (zero, zero), unroll=8)
    h = h_s[0, n_ctx:, :] + h_s[1, n_ctx:, :]
    o_ref[0] = (h * _gelu_tanh(yl_ref[0, n_ctx:, :].astype(F32))).astype(BF16)


def _rglru(proj, lru_conv, lru_conv_b, w_gates, b_gates, lam, n_ctx):
    b_sz, t_all, _ = proj.shape
    n_lat = t_all - n_ctx
    gw = LRU_GROUP
    x_blk = 4 * DN_WIDTH // gw
    y_blk = (4 * DN_WIDTH + LRU_WIDTH) // gw
    return pl.pallas_call(
        functools.partial(_lru_kernel, n_ctx),
        grid=(b_sz, N_LRU_GROUPS),
        in_specs=[pl.BlockSpec((1, t_all, gw), lambda b, g: (b, 0, x_blk + g)),
                  pl.BlockSpec((1, t_all, gw), lambda b, g: (b, 0, y_blk + g)),
                  pl.BlockSpec((lru_conv.shape[0], gw), lambda b, g: (0, g)),
                  pl.BlockSpec((1, gw), lambda b, g: (0, g)),
                  pl.BlockSpec((1, gw, 2 * N_DIRS * gw), lambda b, g: (g, 0, 0)),
                  pl.BlockSpec((1, 1, 2 * N_DIRS * gw), lambda b, g: (g, 0, 0)),
                  pl.BlockSpec((1, 1, N_DIRS * gw), lambda b, g: (g, 0, 0))],
        out_specs=pl.BlockSpec((1, n_lat, gw), lambda b, g: (b, 0, g)),
        out_shape=jax.ShapeDtypeStruct((b_sz, n_lat, LRU_WIDTH), BF16),
        scratch_shapes=[pltpu.VMEM((t_all, gw), F32),
                        pltpu.VMEM((N_DIRS, t_all, gw), F32),
                        pltpu.VMEM((N_DIRS, t_all, gw), F32),
                        pltpu.VMEM((N_DIRS, t_all, gw), F32)],
        compiler_params=pltpu.CompilerParams(dimension_semantics=("arbitrary", "arbitrary"),
                                             vmem_limit_bytes=VMEM_LIMIT),
        name="rglru",
    )(proj, proj, lru_conv, lru_conv_b, w_gates, b_gates, lam)


MERGE_ROWS = 256


def _merge_kernel(ydn_ref, ylru_ref, mg_ref, x_ref, bm_ref, wdn_ref, wlru_ref, wout_ref, gpost_ref,
                  gate_ref, gpre_ref, sh_ref, sc_ref, h_ref, u_ref):
    d = D_MODEL
    gl = jax.nn.sigmoid(mg_ref[0].astype(F32) + bm_ref[...])
    p_dn = jnp.dot(ydn_ref[0], wdn_ref[...], preferred_element_type=F32)
    p_lru = jnp.dot(ylru_ref[0], wlru_ref[...], preferred_element_type=F32)
    mix = _bdot(gl[:, :d] * p_dn + gl[:, d:] * p_lru, wout_ref[...])
    h = x_ref[0] + _rmsnorm(mix, gpost_ref[...]) * gate_ref[0]
    h_ref[0] = h
    u_ref[0] = (_rmsnorm(h, gpre_ref[...]) * (1.0 + sc_ref[0]) + sh_ref[0]).astype(BF16)


def _merge(y_dn, y_lru, proj, x, mod3, b_merge, w_dn, w_lru, w_out, g_post, g_pre_ffn, n_ctx):
    b_sz, n_lat, d = x.shape
    tm = MERGE_ROWS
    row0 = n_ctx // tm
    mg_blk = (4 * DN_WIDTH + 2 * LRU_WIDTH) // (2 * d)
    tile = lambda: pl.BlockSpec((1, tm, d), lambda b, i: (b, i, 0))
    full = lambda r, c: pl.BlockSpec((r, c), lambda b, i: (0, 0))
    vec = lambda k: pl.BlockSpec((1, 1, d), lambda b, i: (b, 0, k))
    return pl.pallas_call(
        _merge_kernel,
        grid=(b_sz, n_lat // tm),
        in_specs=[tile(), tile(),
                  pl.BlockSpec((1, tm, 2 * d), lambda b, i: (b, row0 + i, mg_blk)),
                  tile(), full(1, 2 * d), full(d, d), full(d, d), full(d, d), full(1, d),
                  vec(2), full(1, d), vec(3), vec(4)],
        out_specs=[tile(), tile()],
        out_shape=[jax.ShapeDtypeStruct((b_sz, n_lat, d), F32),
                   jax.ShapeDtypeStruct((b_sz, n_lat, d), BF16)],
        compiler_params=pltpu.CompilerParams(dimension_semantics=("arbitrary", "arbitrary"),
                                             vmem_limit_bytes=VMEM_LIMIT),
        name="merge",
    )(y_dn, y_lru, proj, x, b_merge, w_dn, w_lru, w_out, g_post, mod3, g_pre_ffn, mod3, mod3)


FFN_ROWS = 1024
FFN_TF = 512
FFN_SUB = 128


def _ffn_kernel(u_ref, wg_ref, wv_ref, dw_ref, db_ref, wd_ref, h_ref, gate_ref, gpost_ref, o_ref,
                g_s, val_s, f_s, acc_s):
    n_lat = u_ref.shape[1]
    half = pl.program_id(1)
    f_idx = pl.program_id(2)
    halo = SUBLANES + GRID_W
    r0 = pl.multiple_of(half * FFN_ROWS, FFN_ROWS)

    @pl.when(f_idx == 0)
    def _():
        acc_s[...] = jnp.zeros_like(acc_s)

    wg = wg_ref[...]
    u_main = u_ref[0, pl.ds(r0, FFN_ROWS), :]
    g_s[halo:halo + FFN_ROWS, :] = jnp.dot(u_main, wg, preferred_element_type=F32)
    top0 = pl.multiple_of(jnp.maximum(r0 - GRID_W, 0), GRID_W)
    bot0 = pl.multiple_of(jnp.minimum(r0 + FFN_ROWS, n_lat - GRID_W), GRID_W)
    top = jnp.dot(u_ref[0, pl.ds(top0, GRID_W), :], wg, preferred_element_type=F32)
    bot = jnp.dot(u_ref[0, pl.ds(bot0, GRID_W), :], wg, preferred_element_type=F32)
    g_s[0:SUBLANES, :] = jnp.zeros((SUBLANES, FFN_TF), F32)
    g_s[SUBLANES:halo, :] = jnp.where(r0 > 0, top, 0.0)
    g_s[halo + FFN_ROWS:halo + FFN_ROWS + GRID_W, :] = jnp.where(r0 + FFN_ROWS < n_lat, bot, 0.0)
    g_s[halo + FFN_ROWS + GRID_W:, :] = jnp.zeros((SUBLANES, FFN_TF), F32)
    val_s[...] = jnp.dot(u_main, wv_ref[...], preferred_element_type=F32)

    dw = dw_ref[...]
    bias = db_ref[...]
    col = lax.broadcasted_iota(jnp.int32, (FFN_SUB, 1), 0) % GRID_W
    not_first = col != 0
    not_last = col != GRID_W - 1

    def conv_act(i, carry):
        base = pl.multiple_of(i * FFN_SUB, FFN_SUB)
        acc = jnp.zeros((FFN_SUB, FFN_TF), F32) + bias
        for dr in range(3):
            start = pl.multiple_of(base + halo - SUBLANES + (dr - 1) * GRID_W, SUBLANES)
            win = g_s[pl.ds(start, FFN_SUB + 2 * SUBLANES), :]
            mid = slice(SUBLANES, SUBLANES + FFN_SUB)
            left = jnp.where(not_first, pltpu.roll(win, 1, 0)[mid], 0.0)
            right = jnp.where(not_last, pltpu.roll(win, FFN_SUB + 2 * SUBLANES - 1, 0)[mid], 0.0)
            acc += left * dw[3 * dr:3 * dr + 1]
            acc += win[mid] * dw[3 * dr + 1:3 * dr + 2]
            acc += right * dw[3 * dr + 2:3 * dr + 3]
        rows = pl.ds(base, FFN_SUB)
        f_s[rows, :] = (_gelu_tanh(acc) * val_s[rows, :]).astype(BF16)
        return carry

    lax.fori_loop(0, FFN_ROWS // FFN_SUB, conv_act, 0)
    acc_s[...] += jnp.dot(f_s[...], wd_ref[...], preferred_element_type=F32)

    @pl.when(f_idx == pl.num_programs(2) - 1)
    def _():
        o_ref[0] = h_ref[0] + _rmsnorm(acc_s[...], gpost_ref[...]) * gate_ref[0]


def _ffn(u2, h1, mod3, w_upg, w_upv, dw9, dwb, w_down, g_post):
    b_sz, n_lat, d = h1.shape
    tf = FFN_TF
    return pl.pallas_call(
        _ffn_kernel,
        grid=(b_sz, n_lat // FFN_ROWS, D_FF // tf),
        in_specs=[pl.BlockSpec((1, n_lat, d), lambda b, r, f: (b, 0, 0)),
                  pl.BlockSpec((d, tf), lambda b, r, f: (0, f)),
                  pl.BlockSpec((d, tf), lambda b, r, f: (0, f)),
                  pl.BlockSpec((9, tf), lambda b, r, f: (0, f)),
                  pl.BlockSpec((1, tf), lambda b, r, f: (0, f)),
                  pl.BlockSpec((tf, d), lambda b, r, f: (f, 0)),
                  pl.BlockSpec((1, FFN_ROWS, d), lambda b, r, f: (b, r, 0)),
                  pl.BlockSpec((1, 1, d), lambda b, r, f: (b, 0, 5)),
                  pl.BlockSpec((1, d), lambda b, r, f: (0, 0))],
        out_specs=pl.BlockSpec((1, FFN_ROWS, d), lambda b, r, f: (b, r, 0)),
        out_shape=jax.ShapeDtypeStruct((b_sz, n_lat, d), F32),
        scratch_shapes=[pltpu.VMEM((FFN_ROWS + 2 * (GRID_W + SUBLANES), tf), F32),
                        pltpu.VMEM((FFN_ROWS, tf), F32),
                        pltpu.VMEM((FFN_ROWS, tf), BF16),
                        pltpu.VMEM((FFN_ROWS, d), F32)],
        compiler_params=pltpu.CompilerParams(dimension_semantics=("arbitrary", "arbitrary", "arbitrary"),
                                             vmem_limit_bytes=VMEM_LIMIT),
        name="ffn",
    )(u2, w_upg, w_upv, dw9, dwb, w_down, h1, mod3, g_post)


def _block_diag_groups(w):
    per = LRU_GROUP // LRU_BLOCK_DIM
    w4 = w.reshape(N_LRU_GROUPS, per, LRU_BLOCK_DIM, LRU_BLOCK_DIM)
    eye = jnp.eye(per, dtype=w.dtype)
    return jnp.einsum('gbij,bc->gbicj', w4, eye).reshape(N_LRU_GROUPS, LRU_GROUP, LRU_GROUP)


def kernel(x, c, ctx, c_ctx, w_ada, b_ada, g_pre_mix, g_post_mix, g_pre_ffn, g_post_ffn, w_in, b_merge, dn_conv, dn_a_log, dn_dt_bias, dn_onorm, lru_conv, lru_conv_b, lru_w_rg, lru_b_rg, lru_w_ig, lru_b_ig, lru_lambda, w_branch_dn, w_branch_lru, w_out, w_up, ffn_dw, ffn_dw_b, w_down):
    b_sz, n_lat, d = x.shape
    n_ctx = ctx.shape[1]
    t_all = n_ctx + n_lat
    n_chunks = t_all // CHUNK
    assert w_ada.shape[0] == 1, "single trunk layer"
    assert d == D_MODEL and n_ctx % MERGE_ROWS == 0 and n_lat % FFN_ROWS == 0 and n_ctx % CHUNK == 0

    pad = (-(b_sz + 1)) % SUBLANES
    cc = jnp.concatenate([c, c_ctx[None], jnp.zeros((pad, d), F32)], axis=0)
    mod = _ada(cc, w_ada[0], b_ada)
    mod3 = mod.reshape(mod.shape[0], 1, 6 * d)

    wl = w_in[0]
    ab0 = 4 * DN_WIDTH
    w_main = jnp.concatenate([wl[:, :ab0], wl[:, ab0 + N_AB:]], axis=1).astype(BF16)
    w_ab = jnp.pad(wl[:, ab0:ab0 + N_AB], ((0, 0), (0, LANES - N_AB))).astype(BF16)
    half_ab = N_DIRS * DN_HEADS
    alog = jnp.pad(dn_a_log[0].reshape(1, half_ab), ((0, 0), (0, LANES - half_ab)))
    dtb = jnp.pad(dn_dt_bias[0].reshape(1, half_ab), ((0, 0), (0, LANES - half_ab)))

    proj, gates = _inproj(ctx, x, mod3, g_pre_mix, w_main, w_ab, alog, dtb)

    g4 = gates[:, :, :N_AB].reshape(b_sz, t_all, 2 * N_DIRS, DN_HEADS)
    gcol = jnp.transpose(g4, (0, 3, 1, 2))
    grow = jnp.transpose(g4[:, :, :N_DIRS], (0, 3, 2, 1)).reshape(b_sz, DN_HEADS, N_DIRS, n_chunks, CHUNK)

    y_dn = _deltanet(proj, gcol, grow, dn_conv[0], dn_onorm, n_ctx)

    w_gates = jnp.concatenate([_block_diag_groups(lru_w_rg[0, 0]), _block_diag_groups(lru_w_ig[0, 0]),
                               _block_diag_groups(lru_w_rg[0, 1]), _block_diag_groups(lru_w_ig[0, 1])],
                              axis=-1).astype(BF16)
    grp = lambda v: v.reshape(N_LRU_GROUPS, 1, LRU_GROUP)
    b_gates = jnp.concatenate([grp(lru_b_rg[0, 0]), grp(lru_b_ig[0, 0]),
                               grp(lru_b_rg[0, 1]), grp(lru_b_ig[0, 1])], axis=-1)
    lam = jnp.concatenate([grp(lru_lambda[0, 0]), grp(lru_lambda[0, 1])], axis=-1)
    y_lru = _rglru(proj, lru_conv[0], lru_conv_b, w_gates, b_gates, lam, n_ctx)

    h1, u2 = _merge(y_dn, y_lru, proj, x, mod3, b_merge, w_branch_dn[0].astype(BF16),
                    w_branch_lru[0].astype(BF16), w_out[0].astype(BF16), g_post_mix, g_pre_ffn, n_ctx)

    w_up_l = w_up[0]
    return _ffn(u2, h1, mod3, w_up_l[:, :D_FF].astype(BF16), w_up_l[:, D_FF:].astype(BF16),
                ffn_dw[0].reshape(9, D_FF), ffn_dw_b, w_down[0].astype(BF16), g_post_ffn)
```

```python
import functools

import jax
import jax.numpy as jnp
from jax import lax
from jax.experimental import pallas as pl
from jax.experimental.pallas import tpu as pltpu

F32 = jnp.float32
BF16 = jnp.bfloat16

EPS = 1e-6
D_MODEL = 1024
GRID_W = 64
DN_HEADS = 8
DN_HEAD_DIM = 128
DN_WIDTH = DN_HEADS * DN_HEAD_DIM
CHUNK = 64
LRU_WIDTH = 1024
LRU_BLOCKS = 16
LRU_BLOCK_DIM = LRU_WIDTH // LRU_BLOCKS
LRU_C = 8.0
N_DIRS = 2
D_FF = 4 * D_MODEL
N_AB = 2 * N_DIRS * DN_HEADS
D_MAIN = 4 * DN_WIDTH + 2 * LRU_WIDTH + 2 * D_MODEL

LANES = 128
SUBLANES = 8
LRU_GROUP = 256
N_LRU_GROUPS = LRU_WIDTH // LRU_GROUP
VMEM_LIMIT = 56 * 1024 * 1024


def _bdot(a, b):
    return jnp.dot(a.astype(BF16), b.astype(BF16), preferred_element_type=F32)


def _split(a):
    hi = a.astype(BF16)
    return hi, (a - hi.astype(F32)).astype(BF16)


def _dot3(a, b):
    a_hi, a_lo = _split(a)
    b_hi, b_lo = _split(b)
    d = functools.partial(jnp.dot, preferred_element_type=F32)
    return d(a_hi, b_hi) + (d(a_hi, b_lo) + d(a_lo, b_hi))


def _silu(x):
    return x * jax.nn.sigmoid(x)


def _softplus(x):
    return jnp.maximum(x, 0.0) + jnp.log1p(jnp.exp(-jnp.abs(x)))


def _gelu_tanh(x):
    return 0.5 * x * (1.0 + jnp.tanh(0.7978845608028654 * (x + 0.044715 * (x * x * x))))


def _rmsnorm(v, gain):
    ms = jnp.mean(v * v, axis=-1, keepdims=True)
    return v * lax.rsqrt(ms + EPS) * gain


def _seg_pos(t_all, n_ctx):
    row = lax.broadcasted_iota(jnp.int32, (t_all, 1), 0)
    in_ctx = row < n_ctx
    return jnp.where(in_ctx, row, row - n_ctx), jnp.where(in_ctx, n_ctx, t_all - n_ctx)


def _seg_conv4(x, w, pos, seglen):
    t_all = x.shape[0]
    acc = x * w[2:3]
    acc += jnp.where(pos >= 2, pltpu.roll(x, 2, 0), 0.0) * w[0:1]
    acc += jnp.where(pos >= 1, pltpu.roll(x, 1, 0), 0.0) * w[1:2]
    acc += jnp.where(pos <= seglen - 2, pltpu.roll(x, t_all - 1, 0), 0.0) * w[3:4]
    return acc


def _ada_kernel(c_ref, w_ref, b_ref, o_ref):
    o_ref[...] = _dot3(_silu(c_ref[...]), w_ref[...]) + b_ref[...]


def _ada(cc, w_ada, b_ada):
    rows, d = cc.shape
    n = w_ada.shape[1]
    tn = 1536
    return pl.pallas_call(
        _ada_kernel,
        grid=(n // tn,),
        in_specs=[pl.BlockSpec((rows, d), lambda j: (0, 0)),
                  pl.BlockSpec((d, tn), lambda j: (0, j)),
                  pl.BlockSpec((1, tn), lambda j: (0, j))],
        out_specs=pl.BlockSpec((rows, tn), lambda j: (0, j)),
        out_shape=jax.ShapeDtypeStruct((rows, n), F32),
        compiler_params=pltpu.CompilerParams(dimension_semantics=("arbitrary",),
                                             vmem_limit_bytes=VMEM_LIMIT),
        name="ada",
    )(cc, w_ada, b_ada)


def _chunk_cumsum(g, reverse):
    t_all = g.shape[0]
    pos = lax.broadcasted_iota(jnp.int32, (t_all, 1), 0) % CHUNK
    s = 1
    while s < CHUNK:
        if reverse:
            g = g + jnp.where(pos < CHUNK - s, pltpu.roll(g, t_all - s, 0), 0.0)
        else:
            g = g + jnp.where(pos >= s, pltpu.roll(g, s, 0), 0.0)
        s *= 2
    return g


def _inproj_kernel(n_ctx, ctx_ref, x_ref, shc_ref, scc_ref, shx_ref, scx_ref, g_ref, w_ref, wab_ref,
                   alog_ref, dtb_ref, o_ref, gates_ref, u_ref):
    @pl.when(pl.program_id(1) == 0)
    def _():
        gain = g_ref[...]
        u_ref[0:n_ctx] = (_rmsnorm(ctx_ref[0], gain) * (1.0 + scc_ref[0]) + shc_ref[0]).astype(BF16)
        u_ref[n_ctx:] = (_rmsnorm(x_ref[0], gain) * (1.0 + scx_ref[0]) + shx_ref[0]).astype(BF16)
        ab = jnp.dot(u_ref[...], wab_ref[...], preferred_element_type=F32)
        g = -jnp.exp(alog_ref[...]) * _softplus(ab + dtb_ref[...])
        lane = lax.broadcasted_iota(jnp.int32, (1, LANES), 1)
        gc = jnp.where(lane < DN_HEADS, _chunk_cumsum(g, False), _chunk_cumsum(g, True))
        gates_ref[0] = jnp.where(lane < N_DIRS * DN_HEADS, gc, jax.nn.sigmoid(ab))

    o_ref[0] = jnp.dot(u_ref[...], w_ref[...], preferred_element_type=F32).astype(BF16)


def _inproj(ctx, x, mod3, g_pre, w_main, w_ab, alog, dtb):
    b_sz, n_ctx, d = ctx.shape
    n_lat = x.shape[1]
    t_all = n_ctx + n_lat
    tn = 512
    c_row = b_sz
    vec = lambda k, ctx_row: pl.BlockSpec(
        (1, 1, d), (lambda b, j: (c_row, 0, k)) if ctx_row else (lambda b, j: (b, 0, k)))
    return pl.pallas_call(
        functools.partial(_inproj_kernel, n_ctx),
        grid=(b_sz, D_MAIN // tn),
        in_specs=[pl.BlockSpec((1, n_ctx, d), lambda b, j: (b, 0, 0)),
                  pl.BlockSpec((1, n_lat, d), lambda b, j: (b, 0, 0)),
                  vec(0, True), vec(1, True), vec(0, False), vec(1, False),
                  pl.BlockSpec((1, d), lambda b, j: (0, 0)),
                  pl.BlockSpec((d, tn), lambda b, j: (0, j)),
                  pl.BlockSpec((d, LANES), lambda b, j: (0, 0)),
                  pl.BlockSpec((1, LANES), lambda b, j: (0, 0)),
                  pl.BlockSpec((1, LANES), lambda b, j: (0, 0))],
        out_specs=[pl.BlockSpec((1, t_all, tn), lambda b, j: (b, 0, j)),
                   pl.BlockSpec((1, t_all, LANES), lambda b, j: (b, 0, 0))],
        out_shape=[jax.ShapeDtypeStruct((b_sz, t_all, D_MAIN), BF16),
                   jax.ShapeDtypeStruct((b_sz, t_all, LANES), F32)],
        scratch_shapes=[pltpu.VMEM((t_all, d), BF16)],
        compiler_params=pltpu.CompilerParams(dimension_semantics=("arbitrary", "arbitrary"),
                                             vmem_limit_bytes=VMEM_LIMIT),
        name="inproj",
    )(ctx, x, mod3, mod3, mod3, mod3, g_pre, w_main, w_ab, alog, dtb)


DN_GROUP = 9


def _dn_group_terms(chains, q_s, k_s, v_s):
    dk = DN_HEAD_DIM
    ii = lax.broadcasted_iota(jnp.int32, (CHUNK, CHUNK), 0)
    jj = lax.broadcasted_iota(jnp.int32, (CHUNK, CHUNK), 1)
    eye = (ii == jj).astype(F32)
    nt = (((1,), (1,)), ((), ()))
    tn = (((0,), (0,)), ((), ()))

    scores = []
    for d, rows, gc, beta, gr in chains:
        k = k_s[rows, :]
        kq = jnp.concatenate([k * beta, q_s[rows, :]], axis=0).astype(BF16)
        scores.append(lax.dot_general(kq, k.astype(BF16), nt, preferred_element_type=F32))

    low, attn = [], []
    for (d, rows, gc, beta, gr), s in zip(chains, scores):
        incl = (ii >= jj) if d == 0 else (ii <= jj)
        strict = (ii > jj) if d == 0 else (ii < jj)
        decay = jnp.where(incl, jnp.exp(jnp.minimum(gc - gr, 0.0)), 0.0)
        low.append(jnp.where(strict, s[:CHUNK] * decay, 0.0))
        attn.append((s[CHUNK:] * decay).astype(BF16))

    inv = [eye - l for l in low]
    power = low
    n = 2
    while n < CHUNK:
        power = [_bdot(p, p) for p in power]
        inv = [t + _bdot(t, p) for t, p in zip(inv, power)]
        n *= 2

    wu = []
    for (d, rows, gc, beta, gr), t in zip(chains, inv):
        kb = k_s[rows, :] * beta
        rhs = jnp.concatenate([kb * jnp.exp(gc), v_s[rows, :] * beta], axis=1)
        wu.append(_bdot(t, rhs).astype(BF16))

    au = [jnp.dot(a, w, preferred_element_type=F32) for a, w in zip(attn, wu)]
    ku = []
    for (d, rows, gc, beta, gr), w in zip(chains, wu):
        g_last = gc[CHUNK - 1:CHUNK] if d == 0 else gc[0:1]
        kg = (k_s[rows, :] * jnp.exp(g_last - gc)).astype(BF16)
        ku.append(lax.dot_general(kg, w, tn, preferred_element_type=F32))

    out = []
    for (d, rows, gc, beta, gr), a, kk in zip(chains, au, ku):
        qe = q_s[rows, :] * jnp.exp(gc) - a[:, :dk]
        out.append((-kk[:, :dk], qe, kk[:, dk:], a[:, dk:]))
    return out


def _dn_kernel(n_ctx, q_ref, k_ref, v_ref, z_ref, gcol_ref, grow_ref, wq_ref, wk_ref, wv_ref, on_ref,
               o_ref, q_s, k_s, v_s, mq_s, add_s, o0_s, out_s):
    t_all = q_ref.shape[1]
    n_chunks = t_all // CHUNK
    ctx_chunks = n_ctx // CHUNK
    dk = DN_HEAD_DIM
    pos, seglen = _seg_pos(t_all, n_ctx)

    def conv_silu(ref, w_ref):
        return _silu(_seg_conv4(ref[0].astype(F32), w_ref[...], pos, seglen))

    def l2n(t):
        return t * lax.rsqrt(jnp.sum(t * t, axis=-1, keepdims=True) + EPS)

    q_s[...] = l2n(conv_silu(q_ref, wq_ref)) * (DN_HEAD_DIM ** -0.5)
    k_s[...] = l2n(conv_silu(k_ref, wk_ref))
    v_s[...] = conv_silu(v_ref, wv_ref)

    def pre(grp, carry):
        chains = []
        for j in range(DN_GROUP):
            c = grp * DN_GROUP + j
            rows = pl.ds(pl.multiple_of(c * CHUNK, CHUNK), CHUNK)
            for d in range(N_DIRS):
                chains.append((d, rows, gcol_ref[0, 0, rows, d:d + 1],
                               gcol_ref[0, 0, rows, N_DIRS + d:N_DIRS + d + 1],
                               grow_ref[0, 0, d, pl.ds(c, 1), :], c))
        terms = _dn_group_terms([ch[:5] for ch in chains], q_s, k_s, v_s)
        for (d, rows, _, _, _, c), (trans, qe, add, o0) in zip(chains, terms):
            mq_s[d, c, 0:dk, :] = trans.astype(BF16)
            mq_s[d, c, dk:, :] = qe.astype(BF16)
            add_s[d, c] = add
            o0_s[d, rows, :] = o0
        return carry

    lax.fori_loop(0, n_chunks // DN_GROUP, pre, 0)

    def seq(s, states):
        chunk_of = (s, jnp.where(s < ctx_chunks, ctx_chunks - 1 - s, n_chunks + ctx_chunks - 1 - s))
        new_states = []
        for d in range(N_DIRS):
            c = chunk_of[d]
            rows = pl.ds(pl.multiple_of(c * CHUNK, CHUNK), CHUNK)
            edge = c * CHUNK + (CHUNK - 1 if d == 0 else 0)
            g_last = gcol_ref[0, 0, pl.ds(edge, 1), d:d + 1]
            r = jnp.dot(mq_s[d, c], states[d].astype(BF16), preferred_element_type=F32)
            out_s[d, rows, :] = r[dk:] + o0_s[d, rows, :]
            new_states.append(states[d] * jnp.exp(g_last) + (r[:dk] + add_s[d, c]))
        return tuple(new_states)

    zero = jnp.zeros((dk, dk), F32)
    lax.fori_loop(0, n_chunks, seq, (zero, zero))

    o = out_s[0, n_ctx:, :] + out_s[1, n_ctx:, :]
    y = _rmsnorm(o, on_ref[...]) * _silu(z_ref[0, n_ctx:, :].astype(F32))
    o_ref[0] = y.astype(BF16)


def _deltanet(proj, gcol, grow, dn_conv, onorm, n_ctx):
    b_sz, t_all, _ = proj.shape
    n_lat = t_all - n_ctx
    n_chunks = t_all // CHUNK
    hd = DN_HEAD_DIM
    col = lambda part: pl.BlockSpec((1, t_all, hd), lambda b, h: (b, 0, part * DN_HEADS + h))
    wcol = lambda part: pl.BlockSpec((dn_conv.shape[0], hd), lambda b, h: (0, part * DN_HEADS + h))
    return pl.pallas_call(
        functools.partial(_dn_kernel, n_ctx),
        grid=(b_sz, DN_HEADS),
        in_specs=[col(0), col(1), col(2), col(3),
                  pl.BlockSpec((1, 1, t_all, 2 * N_DIRS), lambda b, h: (b, h, 0, 0)),
                  pl.BlockSpec((1, 1, N_DIRS, n_chunks, CHUNK), lambda b, h: (b, h, 0, 0, 0)),
                  wcol(0), wcol(1), wcol(2),
                  pl.BlockSpec((1, hd), lambda b, h: (0, 0))],
        out_specs=pl.BlockSpec((1, n_lat, hd), lambda b, h: (b, 0, h)),
        out_shape=jax.ShapeDtypeStruct((b_sz, n_lat, DN_WIDTH), BF16),
        scratch_shapes=[pltpu.VMEM((t_all, hd), F32), pltpu.VMEM((t_all, hd), F32),
                        pltpu.VMEM((t_all, hd), F32),
                        pltpu.VMEM((N_DIRS, n_chunks, hd + CHUNK, hd), BF16),
                        pltpu.VMEM((N_DIRS, n_chunks, hd, hd), F32),
                        pltpu.VMEM((N_DIRS, t_all, hd), F32),
                        pltpu.VMEM((N_DIRS, t_all, hd), F32)],
        compiler_params=pltpu.CompilerParams(dimension_semantics=("arbitrary", "arbitrary"),
                                             vmem_limit_bytes=VMEM_LIMIT),
        name="deltanet",
    )(proj, proj, proj, proj, gcol, grow, dn_conv, dn_conv, dn_conv, onorm)


LRU_ROW_TILE = 256


def _lru_kernel(n_ctx, xl_ref, yl_ref, wc_ref, bc_ref, wg_ref, bg_ref, lam_ref, o_ref,
                xc_s, a_s, b_s, h_s):
    t_all = xl_ref.shape[1]
    gw = LRU_GROUP
    pos, seglen = _seg_pos(t_all, n_ctx)
    xc_s[...] = _seg_conv4(xl_ref[0].astype(F32), wc_ref[...], pos, seglen) + bc_ref[...]
    sp = _softplus(-lam_ref[0])
    sub = lax.broadcasted_iota(jnp.int32, (LRU_ROW_TILE, 1), 0) % SUBLANES

    def gates(i, carry):
        rows = pl.ds(pl.multiple_of(i * LRU_ROW_TILE, LRU_ROW_TILE), LRU_ROW_TILE)
        xc = xc_s[rows, :]
        y = jnp.dot(xc.astype(BF16), wg_ref[0], preferred_element_type=F32) + bg_ref[0]
        for d in range(N_DIRS):
            r = jax.nn.sigmoid(y[:, (2 * d) * gw:(2 * d + 1) * gw])
            ig = jax.nn.sigmoid(y[:, (2 * d + 1) * gw:(2 * d + 2) * gw])
            log_a = (-LRU_C) * r * sp[:, d * gw:(d + 1) * gw]
            a = jnp.exp(log_a)
            b = jnp.sqrt(-jnp.tanh(log_a) * (a * a + 1.0)) * (ig * xc)
            s = 1
            while s < SUBLANES:
                if d == 0:
                    ok = sub >= s
                    a_sh, b_sh = pltpu.roll(a, s, 0), pltpu.roll(b, s, 0)
                else:
                    ok = sub < SUBLANES - s
                    a_sh = pltpu.roll(a, LRU_ROW_TILE - s, 0)
                    b_sh = pltpu.roll(b, LRU_ROW_TILE - s, 0)
                b = b + a * jnp.where(ok, b_sh, 0.0)
                a = a * jnp.where(ok, a_sh, 1.0)
                s *= 2
            a_s[d, rows, :] = a
            b_s[d, rows, :] = b
        return carry

    lax.fori_loop(0, t_all // LRU_ROW_TILE, gates, 0)

    n_tiles = t_all // SUBLANES
    ctx_tiles = n_ctx // SUBLANES

    def carry_step(s, carries):
        tile_of = (s, jnp.where(s < ctx_tiles, ctx_tiles - 1 - s, n_tiles + ctx_tiles - 1 - s))
        new = []
        for d in range(N_DIRS):
            rows = pl.ds(pl.multiple_of(tile_of[d] * SUBLANES, SUBLANES), SUBLANES)
            h = b_s[d, rows, :] + a_s[d, rows, :] * carries[d]
            h_s[d, rows, :] = h
            edge = h[SUBLANES - 1:SUBLANES] if d == 0 else h[0:1]
            new.append(jnp.broadcast_to(edge, h.shape))
        return tuple(new)

    zero = jnp.zeros((SUBLANES, gw), F32)
    lax.fori_loop(0, n_tiles, carry_step, (zero, zero), unroll=8)
    h = h_s[0, n_ctx:, :] + h_s[1, n_ctx:, :]
    o_ref[0] = (h * _gelu_tanh(yl_ref[0, n_ctx:, :].astype(F32))).astype(BF16)


def _rglru(proj, lru_conv, lru_conv_b, w_gates, b_gates, lam, n_ctx):
    b_sz, t_all, _ = proj.shape
    n_lat = t_all - n_ctx
    gw = LRU_GROUP
    x_blk = 4 * DN_WIDTH // gw
    y_blk = (4 * DN_WIDTH + LRU_WIDTH) // gw
    return pl.pallas_call(
        functools.partial(_lru_kernel, n_ctx),
        grid=(b_sz, N_LRU_GROUPS),
        in_specs=[pl.BlockSpec((1, t_all, gw), lambda b, g: (b, 0, x_blk + g)),
                  pl.BlockSpec((1, t_all, gw), lambda b, g: (b, 0, y_blk + g)),
                  pl.BlockSpec((lru_conv.shape[0], gw), lambda b, g: (0, g)),
                  pl.BlockSpec((1, gw), lambda b, g: (0, g)),
                  pl.BlockSpec((1, gw, 2 * N_DIRS * gw), lambda b, g: (g, 0, 0)),
                  pl.BlockSpec((1, 1, 2 * N_DIRS * gw), lambda b, g: (g, 0, 0)),
                  pl.BlockSpec((1, 1, N_DIRS * gw), lambda b, g: (g, 0, 0))],
        out_specs=pl.BlockSpec((1, n_lat, gw), lambda b, g: (b, 0, g)),
        out_shape=jax.ShapeDtypeStruct((b_sz, n_lat, LRU_WIDTH), BF16),
        scratch_shapes=[pltpu.VMEM((t_all, gw), F32),
                        pltpu.VMEM((N_DIRS, t_all, gw), F32),
                        pltpu.VMEM((N_DIRS, t_all, gw), F32),
                        pltpu.VMEM((N_DIRS, t_all, gw), F32)],
        compiler_params=pltpu.CompilerParams(dimension_semantics=("arbitrary", "arbitrary"),
                                             vmem_limit_bytes=VMEM_LIMIT),
        name="rglru",
    )(proj, proj, lru_conv, lru_conv_b, w_gates, b_gates, lam)


MERGE_ROWS = 256


def _merge_kernel(ydn_ref, ylru_ref, mg_ref, x_ref, bm_ref, wdn_ref, wlru_ref, wout_ref, gpost_ref,
                  gate_ref, gpre_ref, sh_ref, sc_ref, h_ref, u_ref):
    d = D_MODEL
    gl = jax.nn.sigmoid(mg_ref[0].astype(F32) + bm_ref[...])
    p_dn = jnp.dot(ydn_ref[0], wdn_ref[...], preferred_element_type=F32)
    p_lru = jnp.dot(ylru_ref[0], wlru_ref[...], preferred_element_type=F32)
    mix = _bdot(gl[:, :d] * p_dn + gl[:, d:] * p_lru, wout_ref[...])
    h = x_ref[0] + _rmsnorm(mix, gpost_ref[...]) * gate_ref[0]
    h_ref[0] = h
    u_ref[0] = (_rmsnorm(h, gpre_ref[...]) * (1.0 + sc_ref[0]) + sh_ref[0]).astype(BF16)


def _merge(y_dn, y_lru, proj, x, mod3, b_merge, w_dn, w_lru, w_out, g_post, g_pre_ffn, n_ctx):
    b_sz, n_lat, d = x.shape
    tm = MERGE_ROWS
    row0 = n_ctx // tm
    mg_blk = (4 * DN_WIDTH + 2 * LRU_WIDTH) // (2 * d)
    tile = lambda: pl.BlockSpec((1, tm, d), lambda b, i: (b, i, 0))
    full = lambda r, c: pl.BlockSpec((r, c), lambda b, i: (0, 0))
    vec = lambda k: pl.BlockSpec((1, 1, d), lambda b, i: (b, 0, k))
    return pl.pallas_call(
        _merge_kernel,
        grid=(b_sz, n_lat // tm),
        in_specs=[tile(), tile(),
                  pl.BlockSpec((1, tm, 2 * d), lambda b, i: (b, row0 + i, mg_blk)),
                  tile(), full(1, 2 * d), full(d, d), full(d, d), full(d, d), full(1, d),
                  vec(2), full(1, d), vec(3), vec(4)],
        out_specs=[tile(), tile()],
        out_shape=[jax.ShapeDtypeStruct((b_sz, n_lat, d), F32),
                   jax.ShapeDtypeStruct((b_sz, n_lat, d), BF16)],
        compiler_params=pltpu.CompilerParams(dimension_semantics=("arbitrary", "arbitrary"),
                                             vmem_limit_bytes=VMEM_LIMIT),
        name="merge",
    )(y_dn, y_lru, proj, x, b_merge, w_dn, w_lru, w_out, g_post, mod3, g_pre_ffn, mod3, mod3)


FFN_ROWS = 1024
FFN_TF = 512
FFN_TILES = 2
FFN_SUB = 256
FFN_HALO = SUBLANES + GRID_W


def _ffn_up(t, u_ref, wg_ref, wv_ref, g_s, val_s, r0, n_lat):
    cols = slice(t * FFN_TF, (t + 1) * FFN_TF)
    wg = wg_ref[:, cols]
    u_main = u_ref[0, pl.ds(r0, FFN_ROWS), :]
    g_s[t, FFN_HALO:FFN_HALO + FFN_ROWS, :] = jnp.dot(u_main, wg, preferred_element_type=F32)
    top0 = pl.multiple_of(jnp.maximum(r0 - GRID_W, 0), GRID_W)
    bot0 = pl.multiple_of(jnp.minimum(r0 + FFN_ROWS, n_lat - GRID_W), GRID_W)
    top = jnp.dot(u_ref[0, pl.ds(top0, GRID_W), :], wg, preferred_element_type=F32)
    bot = jnp.dot(u_ref[0, pl.ds(bot0, GRID_W), :], wg, preferred_element_type=F32)
    g_s[t, 0:SUBLANES, :] = jnp.zeros((SUBLANES, FFN_TF), F32)
    g_s[t, SUBLANES:FFN_HALO, :] = jnp.where(r0 > 0, top, 0.0)
    g_s[t, FFN_HALO + FFN_ROWS:FFN_HALO + FFN_ROWS + GRID_W, :] = jnp.where(r0 + FFN_ROWS < n_lat, bot, 0.0)
    g_s[t, FFN_HALO + FFN_ROWS + GRID_W:, :] = jnp.zeros((SUBLANES, FFN_TF), F32)
    val_s[t] = jnp.dot(u_main, wv_ref[:, cols], preferred_element_type=F32)


def _ffn_conv_act(t, dw_ref, db_ref, g_s, val_s, f_s):
    cols = slice(t * FFN_TF, (t + 1) * FFN_TF)
    dw = dw_ref[:, cols]
    bias = db_ref[:, cols]
    ext = FFN_SUB + 2 * FFN_HALO
    col = (lax.broadcasted_iota(jnp.int32, (ext, 1), 0) + (GRID_W - SUBLANES)) % GRID_W
    for i in range(FFN_ROWS // FFN_SUB):
        base = i * FFN_SUB
        win = g_s[t, base:base + ext, :]
        left = jnp.where(col != 0, pltpu.roll(win, 1, 0), 0.0)
        right = jnp.where(col != GRID_W - 1, pltpu.roll(win, ext - 1, 0), 0.0)
        acc = jnp.zeros((FFN_SUB, FFN_TF), F32) + bias
        for dr in range(3):
            rows = slice(SUBLANES + dr * GRID_W, SUBLANES + dr * GRID_W + FFN_SUB)
            acc += left[rows] * dw[3 * dr:3 * dr + 1]
            acc += win[rows] * dw[3 * dr + 1:3 * dr + 2]
            acc += right[rows] * dw[3 * dr + 2:3 * dr + 3]
        out_rows = slice(base, base + FFN_SUB)
        f_s[t, out_rows, :] = (_gelu_tanh(acc) * val_s[t, out_rows, :]).astype(BF16)


def _ffn_kernel(u_ref, wg_ref, wv_ref, dw_ref, db_ref, wd_ref, h_ref, gate_ref, gpost_ref, o_ref,
                g_s, val_s, f_s, acc_s):
    n_lat = u_ref.shape[1]
    f_idx = pl.program_id(2)
    r0 = pl.multiple_of(pl.program_id(1) * FFN_ROWS, FFN_ROWS)

    @pl.when(f_idx == 0)
    def _():
        acc_s[...] = jnp.zeros_like(acc_s)

    for t in range(FFN_TILES):
        _ffn_up(t, u_ref, wg_ref, wv_ref, g_s, val_s, r0, n_lat)
    for t in range(FFN_TILES):
        _ffn_conv_act(t, dw_ref, db_ref, g_s, val_s, f_s)
    down = [jnp.dot(f_s[t], wd_ref[t * FFN_TF:(t + 1) * FFN_TF, :], preferred_element_type=F32)
            for t in range(FFN_TILES)]
    acc_s[...] += sum(down[1:], down[0])

    @pl.when(f_idx == pl.num_programs(2) - 1)
    def _():
        o_ref[0] = h_ref[0] + _rmsnorm(acc_s[...], gpost_ref[...]) * gate_ref[0]


def _ffn(u2, h1, mod3, w_upg, w_upv, dw9, dwb, w_down, g_post):
    b_sz, n_lat, d = h1.shape
    tf = FFN_TF * FFN_TILES
    return pl.pallas_call(
        _ffn_kernel,
        grid=(b_sz, n_lat // FFN_ROWS, D_FF // tf),
        in_specs=[pl.BlockSpec((1, n_lat, d), lambda b, r, f: (b, 0, 0)),
                  pl.BlockSpec((d, tf), lambda b, r, f: (0, f)),
                  pl.BlockSpec((d, tf), lambda b, r, f: (0, f)),
                  pl.BlockSpec((9, tf), lambda b, r, f: (0, f)),
                  pl.BlockSpec((1, tf), lambda b, r, f: (0, f)),
                  pl.BlockSpec((tf, d), lambda b, r, f: (f, 0)),
                  pl.BlockSpec((1, FFN_ROWS, d), lambda b, r, f: (b, r, 0)),
                  pl.BlockSpec((1, 1, d), lambda b, r, f: (b, 0, 5)),
                  pl.BlockSpec((1, d), lambda b, r, f: (0, 0))],
        out_specs=pl.BlockSpec((1, FFN_ROWS, d), lambda b, r, f: (b, r, 0)),
        out_shape=jax.ShapeDtypeStruct((b_sz, n_lat, d), F32),
        scratch_shapes=[pltpu.VMEM((FFN_TILES, FFN_ROWS + 2 * FFN_HALO, FFN_TF), F32),
                        pltpu.VMEM((FFN_TILES, FFN_ROWS, FFN_TF), F32),
                        pltpu.VMEM((FFN_TILES, FFN_ROWS, FFN_TF), BF16),
                        pltpu.VMEM((FFN_ROWS, d), F32)],
        compiler_params=pltpu.CompilerParams(dimension_semantics=("arbitrary", "arbitrary", "arbitrary"),
                                             vmem_limit_bytes=VMEM_LIMIT),
        name="ffn",
    )(u2, w_upg, w_upv, dw9, dwb, w_down, h1, mod3, g_post)


def _block_diag_groups(w):
    per = LRU_GROUP // LRU_BLOCK_DIM
    w4 = w.reshape(N_LRU_GROUPS, per, LRU_BLOCK_DIM, LRU_BLOCK_DIM)
    eye = jnp.eye(per, dtype=w.dtype)
    return jnp.einsum('gbij,bc->gbicj', w4, eye).reshape(N_LRU_GROUPS, LRU_GROUP, LRU_GROUP)


def kernel(x, c, ctx, c_ctx, w_ada, b_ada, g_pre_mix, g_post_mix, g_pre_ffn, g_post_ffn, w_in, b_merge, dn_conv, dn_a_log, dn_dt_bias, dn_onorm, lru_conv, lru_conv_b, lru_w_rg, lru_b_rg, lru_w_ig, lru_b_ig, lru_lambda, w_branch_dn, w_branch_lru, w_out, w_up, ffn_dw, ffn_dw_b, w_down):
    b_sz, n_lat, d = x.shape
    n_ctx = ctx.shape[1]
    t_all = n_ctx + n_lat
    n_chunks = t_all // CHUNK
    assert w_ada.shape[0] == 1, "single trunk layer"
    assert d == D_MODEL and n_ctx % MERGE_ROWS == 0 and n_lat % FFN_ROWS == 0 and n_ctx % CHUNK == 0

    pad = (-(b_sz + 1)) % SUBLANES
    cc = jnp.concatenate([c, c_ctx[None], jnp.zeros((pad, d), F32)], axis=0)
    mod = _ada(cc, w_ada[0], b_ada)
    mod3 = mod.reshape(mod.shape[0], 1, 6 * d)

    wl = w_in[0]
    ab0 = 4 * DN_WIDTH
    w_main = jnp.concatenate([wl[:, :ab0], wl[:, ab0 + N_AB:]], axis=1).astype(BF16)
    w_ab = jnp.pad(wl[:, ab0:ab0 + N_AB], ((0, 0), (0, LANES - N_AB))).astype(BF16)
    half_ab = N_DIRS * DN_HEADS
    alog = jnp.pad(dn_a_log[0].reshape(1, half_ab), ((0, 0), (0, LANES - half_ab)))
    dtb = jnp.pad(dn_dt_bias[0].reshape(1, half_ab), ((0, 0), (0, LANES - half_ab)))

    proj, gates = _inproj(ctx, x, mod3, g_pre_mix, w_main, w_ab, alog, dtb)

    g4 = gates[:, :, :N_AB].reshape(b_sz, t_all, 2 * N_DIRS, DN_HEADS)
    gcol = jnp.transpose(g4, (0, 3, 1, 2))
    grow = jnp.transpose(g4[:, :, :N_DIRS], (0, 3, 2, 1)).reshape(b_sz, DN_HEADS, N_DIRS, n_chunks, CHUNK)

    y_dn = _deltanet(proj, gcol, grow, dn_conv[0], dn_onorm, n_ctx)

    w_gates = jnp.concatenate([_block_diag_groups(lru_w_rg[0, 0]), _block_diag_groups(lru_w_ig[0, 0]),
                               _block_diag_groups(lru_w_rg[0, 1]), _block_diag_groups(lru_w_ig[0, 1])],
                              axis=-1).astype(BF16)
    grp = lambda v: v.reshape(N_LRU_GROUPS, 1, LRU_GROUP)
    b_gates = jnp.concatenate([grp(lru_b_rg[0, 0]), grp(lru_b_ig[0, 0]),
                               grp(lru_b_rg[0, 1]), grp(lru_b_ig[0, 1])], axis=-1)
    lam = jnp.concatenate([grp(lru_lambda[0, 0]), grp(lru_lambda[0, 1])], axis=-1)
    y_lru = _rglru(proj, lru_conv[0], lru_conv_b, w_gates, b_gates, lam, n_ctx)

    h1, u2 = _merge(y_dn, y_lru, proj, x, mod3, b_merge, w_branch_dn[0].astype(BF16),
                    w_branch_lru[0].astype(BF16), w_out[0].astype(BF16), g_post_mix, g_pre_ffn, n_ctx)

    w_up_l = w_up[0]
    return _ffn(u2, h1, mod3, w_up_l[:, :D_FF].astype(BF16), w_up_l[:, D_FF:].astype(BF16),
                ffn_dw[0].reshape(9, D_FF), ffn_dw_b, w_down[0].astype(BF16), g_post_ffn)
```

```python
import functools

import jax
import jax.numpy as jnp
from jax import lax
from jax.experimental import pallas as pl
from jax.experimental.pallas import tpu as pltpu

F32 = jnp.float32
BF16 = jnp.bfloat16

EPS = 1e-6
D_MODEL = 1024
GRID_W = 64
DN_HEADS = 8
DN_HEAD_DIM = 128
DN_WIDTH = DN_HEADS * DN_HEAD_DIM
CHUNK = 64
LRU_WIDTH = 1024
LRU_BLOCKS = 16
LRU_BLOCK_DIM = LRU_WIDTH // LRU_BLOCKS
LRU_C = 8.0
N_DIRS = 2
D_FF = 4 * D_MODEL
N_AB = 2 * N_DIRS * DN_HEADS
D_MAIN = 4 * DN_WIDTH + 2 * LRU_WIDTH + 2 * D_MODEL

LANES = 128
SUBLANES = 8
LRU_GROUP = 256
N_LRU_GROUPS = LRU_WIDTH // LRU_GROUP
VMEM_LIMIT = 56 * 1024 * 1024


def _bdot(a, b):
    return jnp.dot(a.astype(BF16), b.astype(BF16), preferred_element_type=F32)


def _split(a):
    hi = a.astype(BF16)
    return hi, (a - hi.astype(F32)).astype(BF16)


def _dot3(a, b):
    a_hi, a_lo = _split(a)
    b_hi, b_lo = _split(b)
    d = functools.partial(jnp.dot, preferred_element_type=F32)
    return d(a_hi, b_hi) + (d(a_hi, b_lo) + d(a_lo, b_hi))


def _silu(x):
    return x * jax.nn.sigmoid(x)


def _softplus(x):
    return jnp.maximum(x, 0.0) + jnp.log1p(jnp.exp(-jnp.abs(x)))


def _gelu_tanh(x):
    return 0.5 * x * (1.0 + jnp.tanh(0.7978845608028654 * (x + 0.044715 * (x * x * x))))


def _rmsnorm(v, gain):
    ms = jnp.mean(v * v, axis=-1, keepdims=True)
    return v * lax.rsqrt(ms + EPS) * gain


def _seg_pos(t_all, n_ctx):
    row = lax.broadcasted_iota(jnp.int32, (t_all, 1), 0)
    in_ctx = row < n_ctx
    return jnp.where(in_ctx, row, row - n_ctx), jnp.where(in_ctx, n_ctx, t_all - n_ctx)


def _seg_conv4(x, w, pos, seglen):
    t_all = x.shape[0]
    acc = x * w[2:3]
    acc += jnp.where(pos >= 2, pltpu.roll(x, 2, 0), 0.0) * w[0:1]
    acc += jnp.where(pos >= 1, pltpu.roll(x, 1, 0), 0.0) * w[1:2]
    acc += jnp.where(pos <= seglen - 2, pltpu.roll(x, t_all - 1, 0), 0.0) * w[3:4]
    return acc


def _ada_kernel(c_ref, w_ref, b_ref, o_ref):
    o_ref[...] = _dot3(_silu(c_ref[...]), w_ref[...]) + b_ref[...]


def _ada(cc, w_ada, b_ada):
    rows, d = cc.shape
    n = w_ada.shape[1]
    tn = 1536
    return pl.pallas_call(
        _ada_kernel,
        grid=(n // tn,),
        in_specs=[pl.BlockSpec((rows, d), lambda j: (0, 0)),
                  pl.BlockSpec((d, tn), lambda j: (0, j)),
                  pl.BlockSpec((1, tn), lambda j: (0, j))],
        out_specs=pl.BlockSpec((rows, tn), lambda j: (0, j)),
        out_shape=jax.ShapeDtypeStruct((rows, n), F32),
        compiler_params=pltpu.CompilerParams(dimension_semantics=("arbitrary",),
                                             vmem_limit_bytes=VMEM_LIMIT),
        name="ada",
    )(cc, w_ada, b_ada)


def _chunk_cumsum(g, reverse):
    t_all = g.shape[0]
    pos = lax.broadcasted_iota(jnp.int32, (t_all, 1), 0) % CHUNK
    s = 1
    while s < CHUNK:
        if reverse:
            g = g + jnp.where(pos < CHUNK - s, pltpu.roll(g, t_all - s, 0), 0.0)
        else:
            g = g + jnp.where(pos >= s, pltpu.roll(g, s, 0), 0.0)
        s *= 2
    return g


def _inproj_kernel(n_ctx, ctx_ref, x_ref, shc_ref, scc_ref, shx_ref, scx_ref, g_ref, w_ref, wab_ref,
                   alog_ref, dtb_ref, o_ref, gates_ref, gates_t_ref, u_ref):
    @pl.when(pl.program_id(1) == 0)
    def _():
        gain = g_ref[...]
        u_ref[0:n_ctx] = (_rmsnorm(ctx_ref[0], gain) * (1.0 + scc_ref[0]) + shc_ref[0]).astype(BF16)
        u_ref[n_ctx:] = (_rmsnorm(x_ref[0], gain) * (1.0 + scx_ref[0]) + shx_ref[0]).astype(BF16)
        ab = jnp.dot(u_ref[...], wab_ref[...], preferred_element_type=F32)
        g = -jnp.exp(alog_ref[...]) * _softplus(ab + dtb_ref[...])
        lane = lax.broadcasted_iota(jnp.int32, (1, LANES), 1)
        gc = jnp.where(lane < DN_HEADS, _chunk_cumsum(g, False), _chunk_cumsum(g, True))
        gates = jnp.where(lane < N_DIRS * DN_HEADS, gc, jax.nn.sigmoid(ab))
        gates_ref[0] = gates
        gates_t_ref[0] = gates.T[0:N_DIRS * DN_HEADS]

    o_ref[0] = jnp.dot(u_ref[...], w_ref[...], preferred_element_type=F32).astype(BF16)


def _inproj(ctx, x, mod3, g_pre, w_main, w_ab, alog, dtb):
    b_sz, n_ctx, d = ctx.shape
    n_lat = x.shape[1]
    t_all = n_ctx + n_lat
    tn = 1024
    c_row = b_sz
    vec = lambda k, ctx_row: pl.BlockSpec(
        (1, 1, d), (lambda b, j: (c_row, 0, k)) if ctx_row else (lambda b, j: (b, 0, k)))
    return pl.pallas_call(
        functools.partial(_inproj_kernel, n_ctx),
        grid=(b_sz, D_MAIN // tn),
        in_specs=[pl.BlockSpec((1, n_ctx, d), lambda b, j: (b, 0, 0)),
                  pl.BlockSpec((1, n_lat, d), lambda b, j: (b, 0, 0)),
                  vec(0, True), vec(1, True), vec(0, False), vec(1, False),
                  pl.BlockSpec((1, d), lambda b, j: (0, 0)),
                  pl.BlockSpec((d, tn), lambda b, j: (0, j)),
                  pl.BlockSpec((d, LANES), lambda b, j: (0, 0)),
                  pl.BlockSpec((1, LANES), lambda b, j: (0, 0)),
                  pl.BlockSpec((1, LANES), lambda b, j: (0, 0))],
        out_specs=[pl.BlockSpec((1, t_all, tn), lambda b, j: (b, 0, j)),
                   pl.BlockSpec((1, t_all, LANES), lambda b, j: (b, 0, 0)),
                   pl.BlockSpec((1, N_DIRS * DN_HEADS, t_all), lambda b, j: (b, 0, 0))],
        out_shape=[jax.ShapeDtypeStruct((b_sz, t_all, D_MAIN), BF16),
                   jax.ShapeDtypeStruct((b_sz, t_all, LANES), F32),
                   jax.ShapeDtypeStruct((b_sz, N_DIRS * DN_HEADS, t_all), F32)],
        scratch_shapes=[pltpu.VMEM((t_all, d), BF16)],
        compiler_params=pltpu.CompilerParams(dimension_semantics=("arbitrary", "arbitrary"),
                                             vmem_limit_bytes=VMEM_LIMIT),
        name="inproj",
    )(ctx, x, mod3, mod3, mod3, mod3, g_pre, w_main, w_ab, alog, dtb)


DN_GROUP = 9


def _dn_group_terms(chains, q_s, k_s, v_s):
    dk = DN_HEAD_DIM
    ii = lax.broadcasted_iota(jnp.int32, (CHUNK, CHUNK), 0)
    jj = lax.broadcasted_iota(jnp.int32, (CHUNK, CHUNK), 1)
    eye = (ii == jj).astype(F32)
    nt = (((1,), (1,)), ((), ()))
    tn = (((0,), (0,)), ((), ()))

    scores = []
    for d, rows, gc, beta, gr in chains:
        k = k_s[rows, :]
        kq = jnp.concatenate([k * beta, q_s[rows, :]], axis=0).astype(BF16)
        scores.append(lax.dot_general(kq, k.astype(BF16), nt, preferred_element_type=F32))
    yield None

    low, attn = [], []
    for (d, rows, gc, beta, gr), s in zip(chains, scores):
        incl = (ii >= jj) if d == 0 else (ii <= jj)
        strict = (ii > jj) if d == 0 else (ii < jj)
        decay = jnp.where(incl, jnp.exp(jnp.minimum(gc - gr, 0.0)), 0.0)
        low.append(jnp.where(strict, s[:CHUNK] * decay, 0.0))
        attn.append((s[CHUNK:] * decay).astype(BF16))

    inv = [eye - l for l in low]
    power = low
    n = 2
    while n < CHUNK:
        power = [_bdot(p, p) for p in power]
        yield None
        inv = [t + _bdot(t, p) for t, p in zip(inv, power)]
        yield None
        n *= 2

    wu = []
    for (d, rows, gc, beta, gr), t in zip(chains, inv):
        kb = k_s[rows, :] * beta
        rhs = jnp.concatenate([kb * jnp.exp(gc), v_s[rows, :] * beta], axis=1)
        wu.append(_bdot(t, rhs).astype(BF16))
    yield None

    au = [jnp.dot(a, w, preferred_element_type=F32) for a, w in zip(attn, wu)]
    yield None
    ku = []
    for (d, rows, gc, beta, gr), w in zip(chains, wu):
        g_last = gc[CHUNK - 1:CHUNK] if d == 0 else gc[0:1]
        kg = (k_s[rows, :] * jnp.exp(g_last - gc)).astype(BF16)
        ku.append(lax.dot_general(kg, w, tn, preferred_element_type=F32))
    yield None

    out = []
    for (d, rows, gc, beta, gr), a, kk in zip(chains, au, ku):
        qe = q_s[rows, :] * jnp.exp(gc) - a[:, :dk]
        out.append((-kk[:, :dk], qe, kk[:, dk:], a[:, dk:]))
    yield out


def _dn_kernel(n_ctx, q_ref, k_ref, v_ref, z_ref, gates_ref, grow_ref, wq_ref, wk_ref, wv_ref, on_ref,
               o_ref, q_s, k_s, v_s, gcol_s, mq_s, add_s, o0_s, out_s):
    t_all = q_ref.shape[1]
    n_chunks = t_all // CHUNK
    ctx_chunks = n_ctx // CHUNK
    n_trips = n_chunks // DN_GROUP
    dk = DN_HEAD_DIM
    head = pl.program_id(1)
    pos, seglen = _seg_pos(t_all, n_ctx)

    def conv_silu(ref, w_ref):
        return _silu(_seg_conv4(ref[0].astype(F32), w_ref[...], pos, seglen))

    def l2n(t):
        return t * lax.rsqrt(jnp.sum(t * t, axis=-1, keepdims=True) + EPS)

    q_s[...] = l2n(conv_silu(q_ref, wq_ref)) * (DN_HEAD_DIM ** -0.5)
    k_s[...] = l2n(conv_silu(k_ref, wk_ref))
    v_s[...] = conv_silu(v_ref, wv_ref)

    lane = lax.broadcasted_iota(jnp.int32, (1, LANES), 1)
    gates = gates_ref[0]
    for kk in range(2 * N_DIRS):
        gcol_s[:, kk:kk + 1] = jnp.sum(jnp.where(lane == kk * DN_HEADS + head, gates, 0.0),
                                       axis=-1, keepdims=True)

    def chunk_at(d, s):
        if d == 0:
            return s
        if isinstance(s, int):
            return ctx_chunks - 1 - s if s < ctx_chunks else n_chunks + ctx_chunks - 1 - s
        return jnp.where(s < ctx_chunks, ctx_chunks - 1 - s, n_chunks + ctx_chunks - 1 - s)

    def chunk_rows(c):
        start = c * CHUNK
        return pl.ds(start if isinstance(c, int) else pl.multiple_of(start, CHUNK), CHUNK)

    def seq_step(s, states):
        new_states = []
        for d in range(N_DIRS):
            c = chunk_at(d, s)
            rows = chunk_rows(c)
            edge = c * CHUNK + (CHUNK - 1 if d == 0 else 0)
            g_last = gcol_s[pl.ds(edge, 1), d:d + 1]
            r = jnp.dot(mq_s[d, c], states[d].astype(BF16), preferred_element_type=F32)
            out_s[d, rows, :] = r[dk:] + o0_s[d, rows, :]
            new_states.append(states[d] * jnp.exp(g_last) + (r[:dk] + add_s[d, c]))
        return tuple(new_states)

    def trip(g_pre, g_seq, states):
        pending = [] if g_seq is None else [g_seq * DN_GROUP + j for j in range(DN_GROUP)]
        if g_pre is not None:
            chains = []
            for j in range(DN_GROUP):
                s = g_pre * DN_GROUP + j
                for d in range(N_DIRS):
                    c = chunk_at(d, s)
                    rows = chunk_rows(c)
                    chains.append((d, rows, gcol_s[rows, d:d + 1], gcol_s[rows, N_DIRS + d:N_DIRS + d + 1],
                                   grow_ref[0, d * DN_HEADS + head, pl.ds(c, 1), :], c))
            terms = None
            for terms in _dn_group_terms([ch[:5] for ch in chains], q_s, k_s, v_s):
                if terms is None and pending:
                    states = seq_step(pending.pop(0), states)
            for (d, rows, _, _, _, c), (trans, qe, add, o0) in zip(chains, terms):
                mq_s[d, c, 0:dk, :] = trans.astype(BF16)
                mq_s[d, c, dk:, :] = qe.astype(BF16)
                add_s[d, c] = add
                o0_s[d, rows, :] = o0
        for s in pending:
            states = seq_step(s, states)
        return states

    zero = jnp.zeros((dk, dk), F32)
    states = trip(0, None, (zero, zero))
    states = lax.fori_loop(1, n_trips, lambda g, st: trip(g, g - 1, st), states)
    trip(None, n_trips - 1, states)

    o = out_s[0, n_ctx:, :] + out_s[1, n_ctx:, :]
    y = _rmsnorm(o, on_ref[...]) * _silu(z_ref[0, n_ctx:, :].astype(F32))
    o_ref[0] = y.astype(BF16)


def _deltanet(proj, gates, grow, dn_conv, onorm, n_ctx):
    b_sz, t_all, _ = proj.shape
    n_lat = t_all - n_ctx
    n_chunks = t_all // CHUNK
    hd = DN_HEAD_DIM
    col = lambda part: pl.BlockSpec((1, t_all, hd), lambda b, h: (b, 0, part * DN_HEADS + h))
    wcol = lambda part: pl.BlockSpec((dn_conv.shape[0], hd), lambda b, h: (0, part * DN_HEADS + h))
    return pl.pallas_call(
        functools.partial(_dn_kernel, n_ctx),
        grid=(b_sz, DN_HEADS),
        in_specs=[col(0), col(1), col(2), col(3),
                  pl.BlockSpec((1, t_all, LANES), lambda b, h: (b, 0, 0)),
                  pl.BlockSpec((1, N_DIRS * DN_HEADS, n_chunks, CHUNK), lambda b, h: (b, 0, 0, 0)),
                  wcol(0), wcol(1), wcol(2),
                  pl.BlockSpec((1, hd), lambda b, h: (0, 0))],
        out_specs=pl.BlockSpec((1, n_lat, hd), lambda b, h: (b, 0, h)),
        out_shape=jax.ShapeDtypeStruct((b_sz, n_lat, DN_WIDTH), BF16),
        scratch_shapes=[pltpu.VMEM((t_all, hd), F32), pltpu.VMEM((t_all, hd), F32),
                        pltpu.VMEM((t_all, hd), F32),
                        pltpu.VMEM((t_all, 2 * N_DIRS), F32),
                        pltpu.VMEM((N_DIRS, n_chunks, hd + CHUNK, hd), BF16),
                        pltpu.VMEM((N_DIRS, n_chunks, hd, hd), F32),
                        pltpu.VMEM((N_DIRS, t_all, hd), F32),
                        pltpu.VMEM((N_DIRS, t_all, hd), F32)],
        compiler_params=pltpu.CompilerParams(dimension_semantics=("arbitrary", "arbitrary"),
                                             vmem_limit_bytes=VMEM_LIMIT),
        name="deltanet",
    )(proj, proj, proj, proj, gates, grow, dn_conv, dn_conv, dn_conv, onorm)


LRU_ROW_TILE = 256


def _lru_kernel(n_ctx, xl_ref, yl_ref, wc_ref, bc_ref, wg_ref, bg_ref, lam_ref, o_ref,
                xc_s, a_s, b_s, h_s):
    t_all = xl_ref.shape[1]
    gw = LRU_GROUP
    pos, seglen = _seg_pos(t_all, n_ctx)
    xc_s[...] = _seg_conv4(xl_ref[0].astype(F32), wc_ref[...], pos, seglen) + bc_ref[...]
    sp = _softplus(-lam_ref[0])
    n_half = gw // LANES

    def gates(i, carry):
        rows = pl.ds(pl.multiple_of(i * LRU_ROW_TILE, LRU_ROW_TILE), LRU_ROW_TILE)
        xc = xc_s[rows, :]
        y = jnp.dot(xc.astype(BF16), wg_ref[0], preferred_element_type=F32) + bg_ref[0]
        for d in range(N_DIRS):
            r = jax.nn.sigmoid(y[:, (2 * d) * gw:(2 * d + 1) * gw])
            ig = jax.nn.sigmoid(y[:, (2 * d + 1) * gw:(2 * d + 2) * gw])
            log_a = (-LRU_C) * r * sp[:, d * gw:(d + 1) * gw]
            a = jnp.exp(log_a)
            b = jnp.sqrt(-jnp.tanh(log_a) * (a * a + 1.0)) * (ig * xc)
            for hl in range(n_half):
                a_s[d * n_half + hl, rows, :] = a[:, hl * LANES:(hl + 1) * LANES]
                b_s[d * n_half + hl, rows, :] = b[:, hl * LANES:(hl + 1) * LANES]
        n_sub = LRU_ROW_TILE // SUBLANES
        for dh in range(N_DIRS * n_half):
            order = range(SUBLANES) if dh < n_half else range(SUBLANES - 1, -1, -1)
            a_run = b_run = None
            for k in order:
                view = pl.ds(i * LRU_ROW_TILE + k, n_sub, stride=SUBLANES)
                a_k, b_k = a_s[dh, view, :], b_s[dh, view, :]
                if a_run is not None:
                    b_k = a_k * b_run + b_k
                    a_k = a_k * a_run
                    a_s[dh, view, :] = a_k
                    b_s[dh, view, :] = b_k
                a_run, b_run = a_k, b_k
        return carry

    lax.fori_loop(0, t_all // LRU_ROW_TILE, gates, 0)

    n_tiles = t_all // SUBLANES
    ctx_tiles = n_ctx // SUBLANES

    def carry_step(s, carries):
        tile_of = (s, jnp.where(s < ctx_tiles, ctx_tiles - 1 - s, n_tiles + ctx_tiles - 1 - s))
        new = []
        for d in range(N_DIRS):
            rows = pl.ds(pl.multiple_of(tile_of[d] * SUBLANES, SUBLANES), SUBLANES)
            wide = lambda ref: jnp.concatenate([ref[d * n_half + hl, rows, :] for hl in range(n_half)], axis=1)
            h = wide(b_s) + wide(a_s) * carries[d]
            h_s[d, rows, :] = h
            edge = h[SUBLANES - 1:SUBLANES] if d == 0 else h[0:1]
            new.append(jnp.broadcast_to(edge, h.shape))
        return tuple(new)

    zero = jnp.zeros((SUBLANES, gw), F32)
    lax.fori_loop(0, n_tiles, carry_step, (zero, zero), unroll=8)
    h = h_s[0, n_ctx:, :] + h_s[1, n_ctx:, :]
    o_ref[0] = (h * _gelu_tanh(yl_ref[0, n_ctx:, :].astype(F32))).astype(BF16)


def _rglru(proj, lru_conv, lru_conv_b, w_gates, b_gates, lam, n_ctx):
    b_sz, t_all, _ = proj.shape
    n_lat = t_all - n_ctx
    gw = LRU_GROUP
    x_blk = 4 * DN_WIDTH // gw
    y_blk = (4 * DN_WIDTH + LRU_WIDTH) // gw
    return pl.pallas_call(
        functools.partial(_lru_kernel, n_ctx),
        grid=(b_sz, N_LRU_GROUPS),
        in_specs=[pl.BlockSpec((1, t_all, gw), lambda b, g: (b, 0, x_blk + g)),
                  pl.BlockSpec((1, t_all, gw), lambda b, g: (b, 0, y_blk + g)),
                  pl.BlockSpec((lru_conv.shape[0], gw), lambda b, g: (0, g)),
                  pl.BlockSpec((1, gw), lambda b, g: (0, g)),
                  pl.BlockSpec((1, gw, 2 * N_DIRS * gw), lambda b, g: (g, 0, 0)),
                  pl.BlockSpec((1, 1, 2 * N_DIRS * gw), lambda b, g: (g, 0, 0)),
                  pl.BlockSpec((1, 1, N_DIRS * gw), lambda b, g: (g, 0, 0))],
        out_specs=pl.BlockSpec((1, n_lat, gw), lambda b, g: (b, 0, g)),
        out_shape=jax.ShapeDtypeStruct((b_sz, n_lat, LRU_WIDTH), BF16),
        scratch_shapes=[pltpu.VMEM((t_all, gw), F32),
                        pltpu.VMEM((N_DIRS * (gw // LANES), t_all, LANES), F32),
                        pltpu.VMEM((N_DIRS * (gw // LANES), t_all, LANES), F32),
                        pltpu.VMEM((N_DIRS, t_all, gw), F32)],
        compiler_params=pltpu.CompilerParams(dimension_semantics=("arbitrary", "arbitrary"),
                                             vmem_limit_bytes=VMEM_LIMIT),
        name="rglru",
    )(proj, proj, lru_conv, lru_conv_b, w_gates, b_gates, lam)


MERGE_ROWS = 256


def _merge_kernel(ydn_ref, ylru_ref, mg_ref, x_ref, bm_ref, wdn_ref, wlru_ref, wout_ref, gpost_ref,
                  gate_ref, gpre_ref, sh_ref, sc_ref, h_ref, u_ref):
    d = D_MODEL
    gl = jax.nn.sigmoid(mg_ref[0].astype(F32) + bm_ref[...])
    p_dn = jnp.dot(ydn_ref[0], wdn_ref[...], preferred_element_type=F32)
    p_lru = jnp.dot(ylru_ref[0], wlru_ref[...], preferred_element_type=F32)
    mix = _bdot(gl[:, :d] * p_dn + gl[:, d:] * p_lru, wout_ref[...])
    h = x_ref[0] + _rmsnorm(mix, gpost_ref[...]) * gate_ref[0]
    h_ref[0] = h
    u_ref[0] = (_rmsnorm(h, gpre_ref[...]) * (1.0 + sc_ref[0]) + sh_ref[0]).astype(BF16)


def _merge(y_dn, y_lru, proj, x, mod3, b_merge, w_dn, w_lru, w_out, g_post, g_pre_ffn, n_ctx):
    b_sz, n_lat, d = x.shape
    tm = MERGE_ROWS
    row0 = n_ctx // tm
    mg_blk = (4 * DN_WIDTH + 2 * LRU_WIDTH) // (2 * d)
    tile = lambda: pl.BlockSpec((1, tm, d), lambda b, i: (b, i, 0))
    full = lambda r, c: pl.BlockSpec((r, c), lambda b, i: (0, 0))
    vec = lambda k: pl.BlockSpec((1, 1, d), lambda b, i: (b, 0, k))
    return pl.pallas_call(
        _merge_kernel,
        grid=(b_sz, n_lat // tm),
        in_specs=[tile(), tile(),
                  pl.BlockSpec((1, tm, 2 * d), lambda b, i: (b, row0 + i, mg_blk)),
                  tile(), full(1, 2 * d), full(d, d), full(d, d), full(d, d), full(1, d),
                  vec(2), full(1, d), vec(3), vec(4)],
        out_specs=[tile(), tile()],
        out_shape=[jax.ShapeDtypeStruct((b_sz, n_lat, d), F32),
                   jax.ShapeDtypeStruct((b_sz, n_lat, d), BF16)],
        compiler_params=pltpu.CompilerParams(dimension_semantics=("arbitrary", "arbitrary"),
                                             vmem_limit_bytes=VMEM_LIMIT),
        name="merge",
    )(y_dn, y_lru, proj, x, b_merge, w_dn, w_lru, w_out, g_post, mod3, g_pre_ffn, mod3, mod3)


FFN_ROWS = 1024
FFN_TF = 512
FFN_TILES = 2
FFN_SUB = 256
FFN_HALO = SUBLANES + GRID_W


def _ffn_up(t, u_ref, wg_ref, wv_ref, g_s, val_s, r0, n_lat):
    cols = slice(t * FFN_TF, (t + 1) * FFN_TF)
    wg = wg_ref[:, cols]
    u_main = u_ref[0, pl.ds(r0, FFN_ROWS), :]
    g_s[t, FFN_HALO:FFN_HALO + FFN_ROWS, :] = jnp.dot(u_main, wg, preferred_element_type=F32)
    top0 = pl.multiple_of(jnp.maximum(r0 - GRID_W, 0), GRID_W)
    bot0 = pl.multiple_of(jnp.minimum(r0 + FFN_ROWS, n_lat - GRID_W), GRID_W)
    top = jnp.dot(u_ref[0, pl.ds(top0, GRID_W), :], wg, preferred_element_type=F32)
    bot = jnp.dot(u_ref[0, pl.ds(bot0, GRID_W), :], wg, preferred_element_type=F32)
    g_s[t, 0:SUBLANES, :] = jnp.zeros((SUBLANES, FFN_TF), F32)
    g_s[t, SUBLANES:FFN_HALO, :] = jnp.where(r0 > 0, top, 0.0)
    g_s[t, FFN_HALO + FFN_ROWS:FFN_HALO + FFN_ROWS + GRID_W, :] = jnp.where(r0 + FFN_ROWS < n_lat, bot, 0.0)
    g_s[t, FFN_HALO + FFN_ROWS + GRID_W:, :] = jnp.zeros((SUBLANES, FFN_TF), F32)
    val_s[t] = jnp.dot(u_main, wv_ref[:, cols], preferred_element_type=F32)


def _ffn_conv_act(t, dw_ref, db_ref, g_s, val_s, f_s):
    cols = slice(t * FFN_TF, (t + 1) * FFN_TF)
    dw = dw_ref[:, cols]
    bias = db_ref[:, cols]
    ext = FFN_SUB + 2 * FFN_HALO
    col = (lax.broadcasted_iota(jnp.int32, (ext, 1), 0) + (GRID_W - SUBLANES)) % GRID_W
    for i in range(FFN_ROWS // FFN_SUB):
        base = i * FFN_SUB
        win = g_s[t, base:base + ext, :]
        left = jnp.where(col != 0, pltpu.roll(win, 1, 0), 0.0)
        right = jnp.where(col != GRID_W - 1, pltpu.roll(win, ext - 1, 0), 0.0)
        acc = jnp.zeros((FFN_SUB, FFN_TF), F32) + bias
        for dr in range(3):
            rows = slice(SUBLANES + dr * GRID_W, SUBLANES + dr * GRID_W + FFN_SUB)
            acc += left[rows] * dw[3 * dr:3 * dr + 1]
            acc += win[rows] * dw[3 * dr + 1:3 * dr + 2]
            acc += right[rows] * dw[3 * dr + 2:3 * dr + 3]
        out_rows = slice(base, base + FFN_SUB)
        f_s[t, out_rows, :] = (_gelu_tanh(acc) * val_s[t, out_rows, :]).astype(BF16)


def _ffn_kernel(u_ref, wg_ref, wv_ref, dw_ref, db_ref, wd_ref, h_ref, gate_ref, gpost_ref, o_ref,
                g_s, val_s, f_s, acc_s):
    n_lat = u_ref.shape[1]
    f_idx = pl.program_id(2)
    r0 = pl.multiple_of(pl.program_id(1) * FFN_ROWS, FFN_ROWS)

    @pl.when(f_idx == 0)
    def _():
        acc_s[...] = jnp.zeros_like(acc_s)

    for t in range(FFN_TILES):
        _ffn_up(t, u_ref, wg_ref, wv_ref, g_s, val_s, r0, n_lat)
    for t in range(FFN_TILES):
        _ffn_conv_act(t, dw_ref, db_ref, g_s, val_s, f_s)
    down = [jnp.dot(f_s[t], wd_ref[t * FFN_TF:(t + 1) * FFN_TF, :], preferred_element_type=F32)
            for t in range(FFN_TILES)]
    acc_s[...] += sum(down[1:], down[0])

    @pl.when(f_idx == pl.num_programs(2) - 1)
    def _():
        o_ref[0] = h_ref[0] + _rmsnorm(acc_s[...], gpost_ref[...]) * gate_ref[0]


def _ffn(u2, h1, mod3, w_upg, w_upv, dw9, dwb, w_down, g_post):
    b_sz, n_lat, d = h1.shape
    tf = FFN_TF * FFN_TILES
    return pl.pallas_call(
        _ffn_kernel,
        grid=(b_sz, n_lat // FFN_ROWS, D_FF // tf),
        in_specs=[pl.BlockSpec((1, n_lat, d), lambda b, r, f: (b, 0, 0)),
                  pl.BlockSpec((d, tf), lambda b, r, f: (0, f)),
                  pl.BlockSpec((d, tf), lambda b, r, f: (0, f)),
                  pl.BlockSpec((9, tf), lambda b, r, f: (0, f)),
                  pl.BlockSpec((1, tf), lambda b, r, f: (0, f)),
                  pl.BlockSpec((tf, d), lambda b, r, f: (f, 0)),
                  pl.BlockSpec((1, FFN_ROWS, d), lambda b, r, f: (b, r, 0)),
                  pl.BlockSpec((1, 1, d), lambda b, r, f: (b, 0, 5)),
                  pl.BlockSpec((1, d), lambda b, r, f: (0, 0))],
        out_specs=pl.BlockSpec((1, FFN_ROWS, d), lambda b, r, f: (b, r, 0)),
        out_shape=jax.ShapeDtypeStruct((b_sz, n_lat, d), F32),
        scratch_shapes=[pltpu.VMEM((FFN_TILES, FFN_ROWS + 2 * FFN_HALO, FFN_TF), F32),
                        pltpu.VMEM((FFN_TILES, FFN_ROWS, FFN_TF), F32),
                        pltpu.VMEM((FFN_TILES, FFN_ROWS, FFN_TF), BF16),
                        pltpu.VMEM((FFN_ROWS, d), F32)],
        compiler_params=pltpu.CompilerParams(dimension_semantics=("arbitrary", "arbitrary", "arbitrary"),
                                             vmem_limit_bytes=VMEM_LIMIT),
        name="ffn",
    )(u2, w_upg, w_upv, dw9, dwb, w_down, h1, mod3, g_post)


def _block_diag_groups(w):
    per = LRU_GROUP // LRU_BLOCK_DIM
    w4 = w.reshape(N_LRU_GROUPS, per, LRU_BLOCK_DIM, LRU_BLOCK_DIM)
    eye = jnp.eye(per, dtype=w.dtype)
    return jnp.einsum('gbij,bc->gbicj', w4, eye).reshape(N_LRU_GROUPS, LRU_GROUP, LRU_GROUP)


def kernel(x, c, ctx, c_ctx, w_ada, b_ada, g_pre_mix, g_post_mix, g_pre_ffn, g_post_ffn, w_in, b_merge, dn_conv, dn_a_log, dn_dt_bias, dn_onorm, lru_conv, lru_conv_b, lru_w_rg, lru_b_rg, lru_w_ig, lru_b_ig, lru_lambda, w_branch_dn, w_branch_lru, w_out, w_up, ffn_dw, ffn_dw_b, w_down):
    b_sz, n_lat, d = x.shape
    n_ctx = ctx.shape[1]
    t_all = n_ctx + n_lat
    n_chunks = t_all // CHUNK
    assert w_ada.shape[0] == 1, "single trunk layer"
    assert d == D_MODEL and n_ctx % MERGE_ROWS == 0 and n_lat % FFN_ROWS == 0 and n_ctx % CHUNK == 0

    pad = (-(b_sz + 1)) % SUBLANES
    cc = jnp.concatenate([c, c_ctx[None], jnp.zeros((pad, d), F32)], axis=0)
    mod = _ada(cc, w_ada[0], b_ada)
    mod3 = mod.reshape(mod.shape[0], 1, 6 * d)

    wl = w_in[0]
    ab0 = 4 * DN_WIDTH
    w_main = jnp.concatenate([wl[:, :ab0], wl[:, ab0 + N_AB:]], axis=1).astype(BF16)
    w_ab = jnp.pad(wl[:, ab0:ab0 + N_AB], ((0, 0), (0, LANES - N_AB))).astype(BF16)
    half_ab = N_DIRS * DN_HEADS
    alog = jnp.pad(dn_a_log[0].reshape(1, half_ab), ((0, 0), (0, LANES - half_ab)))
    dtb = jnp.pad(dn_dt_bias[0].reshape(1, half_ab), ((0, 0), (0, LANES - half_ab)))

    proj, gates, gates_t = _inproj(ctx, x, mod3, g_pre_mix, w_main, w_ab, alog, dtb)
    grow = gates_t.reshape(b_sz, N_DIRS * DN_HEADS, n_chunks, CHUNK)
    y_dn = _deltanet(proj, gates, grow, dn_conv[0], dn_onorm, n_ctx)

    w_gates = jnp.concatenate([_block_diag_groups(lru_w_rg[0, 0]), _block_diag_groups(lru_w_ig[0, 0]),
                               _block_diag_groups(lru_w_rg[0, 1]), _block_diag_groups(lru_w_ig[0, 1])],
                              axis=-1).astype(BF16)
    grp = lambda v: v.reshape(N_LRU_GROUPS, 1, LRU_GROUP)
    b_gates = jnp.concatenate([grp(lru_b_rg[0, 0]), grp(lru_b_ig[0, 0]),
                               grp(lru_b_rg[0, 1]), grp(lru_b_ig[0, 1])], axis=-1)
    lam = jnp.concatenate([grp(lru_lambda[0, 0]), grp(lru_lambda[0, 1])], axis=-1)
    y_lru = _rglru(proj, lru_conv[0], lru_conv_b, w_gates, b_gates, lam, n_ctx)

    h1, u2 = _merge(y_dn, y_lru, proj, x, mod3, b_merge, w_branch_dn[0].astype(BF16),
                    w_branch_lru[0].astype(BF16), w_out[0].astype(BF16), g_post_mix, g_pre_ffn, n_ctx)

    w_up_l = w_up[0]
    return _ffn(u2, h1, mod3, w_up_l[:, :D_FF].astype(BF16), w_up_l[:, D_FF:].astype(BF16),
                ffn_dw[0].reshape(9, D_FF), ffn_dw_b, w_down[0].astype(BF16), g_post_ffn)
```

```python
import functools

import jax
import jax.numpy as jnp
from jax import lax
from jax.experimental import pallas as pl
from jax.experimental.pallas import tpu as pltpu

F32 = jnp.float32
BF16 = jnp.bfloat16

EPS = 1e-6
D_MODEL = 1024
GRID_W = 64
DN_HEADS = 8
DN_HEAD_DIM = 128
DN_WIDTH = DN_HEADS * DN_HEAD_DIM
CHUNK = 64
LRU_WIDTH = 1024
LRU_BLOCKS = 16
LRU_BLOCK_DIM = LRU_WIDTH // LRU_BLOCKS
LRU_C = 8.0
N_DIRS = 2
D_FF = 4 * D_MODEL
N_AB = 2 * N_DIRS * DN_HEADS
D_MAIN = 4 * DN_WIDTH + 2 * LRU_WIDTH + 2 * D_MODEL

LANES = 128
SUBLANES = 8
BF16_ROWS = 16
LRU_GROUP = 256
N_LRU_GROUPS = LRU_WIDTH // LRU_GROUP
VMEM_LIMIT = 56 * 1024 * 1024


def _bdot(a, b):
    return jnp.dot(a.astype(BF16), b.astype(BF16), preferred_element_type=F32)


def _split(a):
    hi = a.astype(BF16)
    return hi, (a - hi.astype(F32)).astype(BF16)


def _dot3(a, b):
    a_hi, a_lo = _split(a)
    b_hi, b_lo = _split(b)
    d = functools.partial(jnp.dot, preferred_element_type=F32)
    return d(a_hi, b_hi) + (d(a_hi, b_lo) + d(a_lo, b_hi))


def _silu(x):
    return x * jax.nn.sigmoid(x)


def _softplus(x):
    return jnp.maximum(x, 0.0) + jnp.log1p(jnp.exp(-jnp.abs(x)))


def _gelu_tanh(x):
    return 0.5 * x * (1.0 + jnp.tanh(0.7978845608028654 * (x + 0.044715 * (x * x * x))))


def _rmsnorm(v, gain):
    ms = jnp.mean(v * v, axis=-1, keepdims=True)
    return v * lax.rsqrt(ms + EPS) * gain


def _seg_pos(t_all, n_ctx):
    row = lax.broadcasted_iota(jnp.int32, (t_all, 1), 0)
    in_ctx = row < n_ctx
    return jnp.where(in_ctx, row, row - n_ctx), jnp.where(in_ctx, n_ctx, t_all - n_ctx)


def _seg_conv4(x, w, pos, seglen):
    t_all = x.shape[0]
    acc = x * w[2:3]
    acc += jnp.where(pos >= 2, pltpu.roll(x, 2, 0), 0.0) * w[0:1]
    acc += jnp.where(pos >= 1, pltpu.roll(x, 1, 0), 0.0) * w[1:2]
    acc += jnp.where(pos <= seglen - 2, pltpu.roll(x, t_all - 1, 0), 0.0) * w[3:4]
    return acc


def _ada_kernel(c_ref, w_ref, b_ref, o_ref):
    o_ref[...] = _dot3(_silu(c_ref[...]), w_ref[...]) + b_ref[...]


def _ada(cc, w_ada, b_ada):
    rows, d = cc.shape
    n = w_ada.shape[1]
    tn = 1536
    return pl.pallas_call(
        _ada_kernel,
        grid=(n // tn,),
        in_specs=[pl.BlockSpec((rows, d), lambda j: (0, 0)),
                  pl.BlockSpec((d, tn), lambda j: (0, j)),
                  pl.BlockSpec((1, tn), lambda j: (0, j))],
        out_specs=pl.BlockSpec((rows, tn), lambda j: (0, j)),
        out_shape=jax.ShapeDtypeStruct((rows, n), F32),
        compiler_params=pltpu.CompilerParams(dimension_semantics=("arbitrary",),
                                             vmem_limit_bytes=VMEM_LIMIT),
        name="ada",
    )(cc, w_ada, b_ada)


def _chunk_cumsum(g, reverse):
    t_all = g.shape[0]
    pos = lax.broadcasted_iota(jnp.int32, (t_all, 1), 0) % CHUNK
    s = 1
    while s < CHUNK:
        if reverse:
            g = g + jnp.where(pos < CHUNK - s, pltpu.roll(g, t_all - s, 0), 0.0)
        else:
            g = g + jnp.where(pos >= s, pltpu.roll(g, s, 0), 0.0)
        s *= 2
    return g


def _inproj_kernel(n_ctx, ctx_ref, x_ref, shc_ref, scc_ref, shx_ref, scx_ref, g_ref, w_ref, wab_ref,
                   alog_ref, dtb_ref, o_ref, gates_ref, gates_t_ref, u_ref):
    @pl.when(pl.program_id(1) == 0)
    def _():
        gain = g_ref[...]
        u_ref[0:n_ctx] = (_rmsnorm(ctx_ref[0], gain) * (1.0 + scc_ref[0]) + shc_ref[0]).astype(BF16)
        u_ref[n_ctx:] = (_rmsnorm(x_ref[0], gain) * (1.0 + scx_ref[0]) + shx_ref[0]).astype(BF16)
        ab = jnp.dot(u_ref[...], wab_ref[...], preferred_element_type=F32)
        g = -jnp.exp(alog_ref[...]) * _softplus(ab + dtb_ref[...])
        lane = lax.broadcasted_iota(jnp.int32, (1, LANES), 1)
        gc = jnp.where(lane < DN_HEADS, _chunk_cumsum(g, False), _chunk_cumsum(g, True))
        gates = jnp.where(lane < N_DIRS * DN_HEADS, gc, jax.nn.sigmoid(ab))
        gates_ref[0] = gates
        gates_t_ref[0] = gates.T[0:N_DIRS * DN_HEADS]

    o_ref[0] = jnp.dot(u_ref[...], w_ref[...], preferred_element_type=F32).astype(BF16)


def _inproj(ctx, x, mod3, g_pre, w_main, w_ab, alog, dtb):
    b_sz, n_ctx, d = ctx.shape
    n_lat = x.shape[1]
    t_all = n_ctx + n_lat
    tn = 1024
    c_row = b_sz
    vec = lambda k, ctx_row: pl.BlockSpec(
        (1, 1, d), (lambda b, j: (c_row, 0, k)) if ctx_row else (lambda b, j: (b, 0, k)))
    return pl.pallas_call(
        functools.partial(_inproj_kernel, n_ctx),
        grid=(b_sz, D_MAIN // tn),
        in_specs=[pl.BlockSpec((1, n_ctx, d), lambda b, j: (b, 0, 0)),
                  pl.BlockSpec((1, n_lat, d), lambda b, j: (b, 0, 0)),
                  vec(0, True), vec(1, True), vec(0, False), vec(1, False),
                  pl.BlockSpec((1, d), lambda b, j: (0, 0)),
                  pl.BlockSpec((d, tn), lambda b, j: (0, j)),
                  pl.BlockSpec((d, LANES), lambda b, j: (0, 0)),
                  pl.BlockSpec((1, LANES), lambda b, j: (0, 0)),
                  pl.BlockSpec((1, LANES), lambda b, j: (0, 0))],
        out_specs=[pl.BlockSpec((1, t_all, tn), lambda b, j: (b, 0, j)),
                   pl.BlockSpec((1, t_all, LANES), lambda b, j: (b, 0, 0)),
                   pl.BlockSpec((1, N_DIRS * DN_HEADS, t_all), lambda b, j: (b, 0, 0))],
        out_shape=[jax.ShapeDtypeStruct((b_sz, t_all, D_MAIN), BF16),
                   jax.ShapeDtypeStruct((b_sz, t_all, LANES), F32),
                   jax.ShapeDtypeStruct((b_sz, N_DIRS * DN_HEADS, t_all), F32)],
        scratch_shapes=[pltpu.VMEM((t_all, d), BF16)],
        compiler_params=pltpu.CompilerParams(dimension_semantics=("arbitrary", "arbitrary"),
                                             vmem_limit_bytes=VMEM_LIMIT),
        name="inproj",
    )(ctx, x, mod3, mod3, mod3, mod3, g_pre, w_main, w_ab, alog, dtb)


DN_GROUP = 9


def _dn_group_terms(chains, q_s, k_s, v_s):
    dk = DN_HEAD_DIM
    ii = lax.broadcasted_iota(jnp.int32, (CHUNK, CHUNK), 0)
    jj = lax.broadcasted_iota(jnp.int32, (CHUNK, CHUNK), 1)
    eye = (ii == jj).astype(F32)
    nt = (((1,), (1,)), ((), ()))
    tn = (((0,), (0,)), ((), ()))

    scores = []
    for d, rows, gc, beta, gr in chains:
        k = k_s[rows, :]
        kq = jnp.concatenate([k * beta, q_s[rows, :]], axis=0).astype(BF16)
        scores.append(lax.dot_general(kq, k.astype(BF16), nt, preferred_element_type=F32))
    yield None

    low, attn = [], []
    for (d, rows, gc, beta, gr), s in zip(chains, scores):
        incl = (ii >= jj) if d == 0 else (ii <= jj)
        strict = (ii > jj) if d == 0 else (ii < jj)
        decay = jnp.where(incl, jnp.exp(jnp.minimum(gc - gr, 0.0)), 0.0)
        low.append(jnp.where(strict, s[:CHUNK] * decay, 0.0))
        attn.append((s[CHUNK:] * decay).astype(BF16))

    def joins(m):
        return ((ii // (2 * m)) == (jj // (2 * m))) & ((ii // m) != (jj // m))

    inv = [eye - jnp.where(joins(1), l, 0.0) for l in low]
    m = 2
    while m < CHUNK:
        mask = joins(m)
        half = [_bdot(t, jnp.where(mask, l, 0.0)) for t, l in zip(inv, low)]
        yield None
        inv = [t - _bdot(hf, t) for t, hf in zip(inv, half)]
        yield None
        m *= 2

    wu = []
    for (d, rows, gc, beta, gr), t in zip(chains, inv):
        kb = k_s[rows, :] * beta
        rhs = jnp.concatenate([kb * jnp.exp(gc), v_s[rows, :] * beta], axis=1)
        wu.append(_bdot(t, rhs).astype(BF16))
    yield None

    au = [jnp.dot(a, w, preferred_element_type=F32) for a, w in zip(attn, wu)]
    yield None
    ku = []
    for (d, rows, gc, beta, gr), w in zip(chains, wu):
        g_last = gc[CHUNK - 1:CHUNK] if d == 0 else gc[0:1]
        kg = (k_s[rows, :] * jnp.exp(g_last - gc)).astype(BF16)
        ku.append(lax.dot_general(kg, w, tn, preferred_element_type=F32))
    yield None

    out = []
    for (d, rows, gc, beta, gr), a, kk in zip(chains, au, ku):
        qe = q_s[rows, :] * jnp.exp(gc) - a[:, :dk]
        out.append((-kk[:, :dk], qe, kk[:, dk:], a[:, dk:]))
    yield out


def _dn_kernel(n_ctx, q_ref, k_ref, v_ref, z_ref, gates_ref, grow_ref, wq_ref, wk_ref, wv_ref, on_ref,
               o_ref, q_s, k_s, v_s, gcol_s, mq_s, add_s, o0_s, out_s):
    t_all = q_ref.shape[1]
    n_chunks = t_all // CHUNK
    ctx_chunks = n_ctx // CHUNK
    n_trips = n_chunks // DN_GROUP
    dk = DN_HEAD_DIM
    head = pl.program_id(1)
    gate_col = lambda rows, kk: gcol_s[rows, kk * DN_HEADS:kk * DN_HEADS + 1]

    def conv_silu(ref, w_ref, lo_out, n_out):
        lo, hi = max(lo_out - BF16_ROWS, 0), min(lo_out + n_out + BF16_ROWS, t_all)
        m = hi - lo
        x = ref[0, lo:hi, :].astype(F32)
        w = w_ref[...]
        touches = lambda edge: lo <= edge + 2 and hi >= edge - 2
        if touches(0) or touches(n_ctx) or touches(t_all):
            row = lax.broadcasted_iota(jnp.int32, (m, 1), 0) + lo
            in_ctx = row < n_ctx
            p = jnp.where(in_ctx, row, row - n_ctx)
            last = jnp.where(in_ctx, n_ctx, t_all - n_ctx) - 1
            tap = lambda shift, ok: jnp.where(ok, pltpu.roll(x, shift, 0), 0.0)
            taps = (tap(2, p >= 2), tap(1, p >= 1), tap(m - 1, p < last))
        else:
            taps = (pltpu.roll(x, 2, 0), pltpu.roll(x, 1, 0), pltpu.roll(x, m - 1, 0))
        acc = x * w[2:3] + taps[0] * w[0:1] + taps[1] * w[1:2] + taps[2] * w[3:4]
        return _silu(acc)[lo_out - lo:lo_out - lo + n_out]

    def l2n(t):
        return t * lax.rsqrt(jnp.sum(t * t, axis=-1, keepdims=True) + EPS)

    def prepare(lo, n):
        rows = slice(lo, lo + n)
        q_s[rows, :] = l2n(conv_silu(q_ref, wq_ref, lo, n)) * (DN_HEAD_DIM ** -0.5)
        k_s[rows, :] = l2n(conv_silu(k_ref, wk_ref, lo, n))
        v_s[rows, :] = conv_silu(v_ref, wv_ref, lo, n)
        gcol_s[rows, :] = pltpu.roll(gates_ref[0, rows, :], (LANES - head) % LANES, 1)

    first_f = DN_GROUP * CHUNK
    first_b = (n_chunks - (DN_GROUP - ctx_chunks)) * CHUNK
    blocks = [(0, first_f // 2), (first_f // 2, first_f // 2), (first_b, t_all - first_b)]
    n_mid = 4
    mid = (first_b - first_f) // n_mid
    blocks += [(first_f + i * mid, mid) for i in range(n_mid)]
    for lo, n in blocks:
        prepare(lo, n)

    def chunk_at(d, s):
        if d == 0:
            return s
        if isinstance(s, int):
            return ctx_chunks - 1 - s if s < ctx_chunks else n_chunks + ctx_chunks - 1 - s
        return jnp.where(s < ctx_chunks, ctx_chunks - 1 - s, n_chunks + ctx_chunks - 1 - s)

    def chunk_rows(c):
        start = c * CHUNK
        return pl.ds(start if isinstance(c, int) else pl.multiple_of(start, CHUNK), CHUNK)

    def seq_step(s, states):
        new_states = []
        for d in range(N_DIRS):
            c = chunk_at(d, s)
            rows = chunk_rows(c)
            edge = c * CHUNK + (CHUNK - 1 if d == 0 else 0)
            g_last = gate_col(pl.ds(edge, 1), d)
            r = jnp.dot(mq_s[d, c], states[d].astype(BF16), preferred_element_type=F32)
            out_s[d, rows, :] = r[dk:] + o0_s[d, rows, :]
            new_states.append(states[d] * jnp.exp(g_last) + (r[:dk] + add_s[d, c]))
        return tuple(new_states)

    def trip(g_pre, g_seq, states):
        pending = [] if g_seq is None else [g_seq * DN_GROUP + j for j in range(DN_GROUP)]
        if g_pre is not None:
            chains = []
            for j in range(DN_GROUP):
                s = g_pre * DN_GROUP + j
                for d in range(N_DIRS):
                    c = chunk_at(d, s)
                    rows = chunk_rows(c)
                    chains.append((d, rows, gate_col(rows, d), gate_col(rows, N_DIRS + d),
                                   grow_ref[0, d * DN_HEADS + head, pl.ds(c, 1), :], c))
            terms = None
            for terms in _dn_group_terms([ch[:5] for ch in chains], q_s, k_s, v_s):
                if terms is None and pending:
                    states = seq_step(pending.pop(0), states)
            for (d, rows, _, _, _, c), (trans, qe, add, o0) in zip(chains, terms):
                mq_s[d, c, 0:dk, :] = trans.astype(BF16)
                mq_s[d, c, dk:, :] = qe.astype(BF16)
                add_s[d, c] = add
                o0_s[d, rows, :] = o0
        for s in pending:
            states = seq_step(s, states)
        return states

    zero = jnp.zeros((dk, dk), F32)
    states = trip(0, None, (zero, zero))
    states = lax.fori_loop(1, n_trips, lambda g, st: trip(g, g - 1, st), states)

    def finish(lo, hi):
        rows = slice(lo * CHUNK, hi * CHUNK)
        o = out_s[0, rows, :] + out_s[1, rows, :]
        y = _rmsnorm(o, on_ref[...]) * _silu(z_ref[0, rows, :].astype(F32))
        o_ref[0, lo * CHUNK - n_ctx:hi * CHUNK - n_ctx, :] = y.astype(BF16)

    done_lo = n_chunks + ctx_chunks - (n_trips - 1) * DN_GROUP
    done_hi = (n_trips - 1) * DN_GROUP
    finish(done_lo, done_hi)
    trip(None, n_trips - 1, states)
    finish(ctx_chunks, done_lo)
    finish(done_hi, n_chunks)


def _deltanet(proj, gates, grow, dn_conv, onorm, n_ctx):
    b_sz, t_all, _ = proj.shape
    n_lat = t_all - n_ctx
    n_chunks = t_all // CHUNK
    hd = DN_HEAD_DIM
    col = lambda part: pl.BlockSpec((1, t_all, hd), lambda b, h: (b, 0, part * DN_HEADS + h))
    wcol = lambda part: pl.BlockSpec((dn_conv.shape[0], hd), lambda b, h: (0, part * DN_HEADS + h))
    return pl.pallas_call(
        functools.partial(_dn_kernel, n_ctx),
        grid=(b_sz, DN_HEADS),
        in_specs=[col(0), col(1), col(2), col(3),
                  pl.BlockSpec((1, t_all, LANES), lambda b, h: (b, 0, 0)),
                  pl.BlockSpec((1, N_DIRS * DN_HEADS, n_chunks, CHUNK), lambda b, h: (b, 0, 0, 0)),
                  wcol(0), wcol(1), wcol(2),
                  pl.BlockSpec((1, hd), lambda b, h: (0, 0))],
        out_specs=pl.BlockSpec((1, n_lat, hd), lambda b, h: (b, 0, h)),
        out_shape=jax.ShapeDtypeStruct((b_sz, n_lat, DN_WIDTH), BF16),
        scratch_shapes=[pltpu.VMEM((t_all, hd), F32), pltpu.VMEM((t_all, hd), F32),
                        pltpu.VMEM((t_all, hd), F32),
                        pltpu.VMEM((t_all, LANES), F32),
                        pltpu.VMEM((N_DIRS, n_chunks, hd + CHUNK, hd), BF16),
                        pltpu.VMEM((N_DIRS, n_chunks, hd, hd), F32),
                        pltpu.VMEM((N_DIRS, t_all, hd), F32),
                        pltpu.VMEM((N_DIRS, t_all, hd), F32)],
        compiler_params=pltpu.CompilerParams(dimension_semantics=("arbitrary", "arbitrary"),
                                             vmem_limit_bytes=VMEM_LIMIT),
        name="deltanet",
    )(proj, proj, proj, proj, gates, grow, dn_conv, dn_conv, dn_conv, onorm)


LRU_ROW_TILE = 256


def _lru_kernel(n_ctx, xl_ref, yl_ref, wc_ref, bc_ref, wg_ref, bg_ref, lam_ref, o_ref,
                xc_s, a_s, b_s, h_s):
    t_all = xl_ref.shape[1]
    gw = LRU_GROUP
    pos, seglen = _seg_pos(t_all, n_ctx)
    xc_s[...] = _seg_conv4(xl_ref[0].astype(F32), wc_ref[...], pos, seglen) + bc_ref[...]
    sp = _softplus(-lam_ref[0])
    n_half = gw // LANES

    def gates(i, carry):
        rows = pl.ds(pl.multiple_of(i * LRU_ROW_TILE, LRU_ROW_TILE), LRU_ROW_TILE)
        xc = xc_s[rows, :]
        y = jnp.dot(xc.astype(BF16), wg_ref[0], preferred_element_type=F32) + bg_ref[0]
        for d in range(N_DIRS):
            r = jax.nn.sigmoid(y[:, (2 * d) * gw:(2 * d + 1) * gw])
            ig = jax.nn.sigmoid(y[:, (2 * d + 1) * gw:(2 * d + 2) * gw])
            log_a = (-LRU_C) * r * sp[:, d * gw:(d + 1) * gw]
            a = jnp.exp(log_a)
            b = jnp.sqrt(-jnp.tanh(log_a) * (a * a + 1.0)) * (ig * xc)
            for hl in range(n_half):
                a_s[d * n_half + hl, rows, :] = a[:, hl * LANES:(hl + 1) * LANES]
                b_s[d * n_half + hl, rows, :] = b[:, hl * LANES:(hl + 1) * LANES]
        n_sub = LRU_ROW_TILE // SUBLANES
        for dh in range(N_DIRS * n_half):
            order = range(SUBLANES) if dh < n_half else range(SUBLANES - 1, -1, -1)
            a_run = b_run = None
            for k in order:
                view = pl.ds(i * LRU_ROW_TILE + k, n_sub, stride=SUBLANES)
                a_k, b_k = a_s[dh, view, :], b_s[dh, view, :]
                if a_run is not None:
                    b_k = a_k * b_run + b_k
                    a_k = a_k * a_run
                    a_s[dh, view, :] = a_k
                    b_s[dh, view, :] = b_k
                a_run, b_run = a_k, b_k
        return carry

    lax.fori_loop(0, t_all // LRU_ROW_TILE, gates, 0)

    n_tiles = t_all // SUBLANES
    ctx_tiles = n_ctx // SUBLANES

    def carry_step(s, carries):
        tile_of = (s, jnp.where(s < ctx_tiles, ctx_tiles - 1 - s, n_tiles + ctx_tiles - 1 - s))
        new = []
        for d in range(N_DIRS):
            rows = pl.ds(pl.multiple_of(tile_of[d] * SUBLANES, SUBLANES), SUBLANES)
            wide = lambda ref: jnp.concatenate([ref[d * n_half + hl, rows, :] for hl in range(n_half)], axis=1)
            h = wide(b_s) + wide(a_s) * carries[d]
            h_s[d, rows, :] = h
            edge = h[SUBLANES - 1:SUBLANES] if d == 0 else h[0:1]
            new.append(jnp.broadcast_to(edge, h.shape))
        return tuple(new)

    zero = jnp.zeros((SUBLANES, gw), F32)
    lax.fori_loop(0, n_tiles, carry_step, (zero, zero), unroll=8)
    h = h_s[0, n_ctx:, :] + h_s[1, n_ctx:, :]
    o_ref[0] = (h * _gelu_tanh(yl_ref[0, n_ctx:, :].astype(F32))).astype(BF16)


def _rglru(proj, lru_conv, lru_conv_b, w_gates, b_gates, lam, n_ctx):
    b_sz, t_all, _ = proj.shape
    n_lat = t_all - n_ctx
    gw = LRU_GROUP
    x_blk = 4 * DN_WIDTH // gw
    y_blk = (4 * DN_WIDTH + LRU_WIDTH) // gw
    return pl.pallas_call(
        functools.partial(_lru_kernel, n_ctx),
        grid=(b_sz, N_LRU_GROUPS),
        in_specs=[pl.BlockSpec((1, t_all, gw), lambda b, g: (b, 0, x_blk + g)),
                  pl.BlockSpec((1, t_all, gw), lambda b, g: (b, 0, y_blk + g)),
                  pl.BlockSpec((lru_conv.shape[0], gw), lambda b, g: (0, g)),
                  pl.BlockSpec((1, gw), lambda b, g: (0, g)),
                  pl.BlockSpec((1, gw, 2 * N_DIRS * gw), lambda b, g: (g, 0, 0)),
                  pl.BlockSpec((1, 1, 2 * N_DIRS * gw), lambda b, g: (g, 0, 0)),
                  pl.BlockSpec((1, 1, N_DIRS * gw), lambda b, g: (g, 0, 0))],
        out_specs=pl.BlockSpec((1, n_lat, gw), lambda b, g: (b, 0, g)),
        out_shape=jax.ShapeDtypeStruct((b_sz, n_lat, LRU_WIDTH), BF16),
        scratch_shapes=[pltpu.VMEM((t_all, gw), F32),
                        pltpu.VMEM((N_DIRS * (gw // LANES), t_all, LANES), F32),
                        pltpu.VMEM((N_DIRS * (gw // LANES), t_all, LANES), F32),
                        pltpu.VMEM((N_DIRS, t_all, gw), F32)],
        compiler_params=pltpu.CompilerParams(dimension_semantics=("arbitrary", "arbitrary"),
                                             vmem_limit_bytes=VMEM_LIMIT),
        name="rglru",
    )(proj, proj, lru_conv, lru_conv_b, w_gates, b_gates, lam)


MERGE_ROWS = 256


def _merge_kernel(ydn_ref, ylru_ref, mg_ref, x_ref, bm_ref, wdn_ref, wlru_ref, wout_ref, gpost_ref,
                  gate_ref, gpre_ref, sh_ref, sc_ref, h_ref, u_ref):
    d = D_MODEL
    gl = jax.nn.sigmoid(mg_ref[0].astype(F32) + bm_ref[...])
    p_dn = jnp.dot(ydn_ref[0], wdn_ref[...], preferred_element_type=F32)
    p_lru = jnp.dot(ylru_ref[0], wlru_ref[...], preferred_element_type=F32)
    mix = _bdot(gl[:, :d] * p_dn + gl[:, d:] * p_lru, wout_ref[...])
    h = x_ref[0] + _rmsnorm(mix, gpost_ref[...]) * gate_ref[0]
    h_ref[0] = h
    u_ref[0] = (_rmsnorm(h, gpre_ref[...]) * (1.0 + sc_ref[0]) + sh_ref[0]).astype(BF16)


def _merge(y_dn, y_lru, proj, x, mod3, b_merge, w_dn, w_lru, w_out, g_post, g_pre_ffn, n_ctx):
    b_sz, n_lat, d = x.shape
    tm = MERGE_ROWS
    row0 = n_ctx // tm
    mg_blk = (4 * DN_WIDTH + 2 * LRU_WIDTH) // (2 * d)
    tile = lambda: pl.BlockSpec((1, tm, d), lambda b, i: (b, i, 0))
    full = lambda r, c: pl.BlockSpec((r, c), lambda b, i: (0, 0))
    vec = lambda k: pl.BlockSpec((1, 1, d), lambda b, i: (b, 0, k))
    return pl.pallas_call(
        _merge_kernel,
        grid=(b_sz, n_lat // tm),
        in_specs=[tile(), tile(),
                  pl.BlockSpec((1, tm, 2 * d), lambda b, i: (b, row0 + i, mg_blk)),
                  tile(), full(1, 2 * d), full(d, d), full(d, d), full(d, d), full(1, d),
                  vec(2), full(1, d), vec(3), vec(4)],
        out_specs=[tile(), tile()],
        out_shape=[jax.ShapeDtypeStruct((b_sz, n_lat, d), F32),
                   jax.ShapeDtypeStruct((b_sz, n_lat, d), BF16)],
        compiler_params=pltpu.CompilerParams(dimension_semantics=("arbitrary", "arbitrary"),
                                             vmem_limit_bytes=VMEM_LIMIT),
        name="merge",
    )(y_dn, y_lru, proj, x, b_merge, w_dn, w_lru, w_out, g_post, mod3, g_pre_ffn, mod3, mod3)


FFN_ROWS = 1024
FFN_TF = 512
FFN_TILES = 2
FFN_SUB = 256
FFN_HALO = SUBLANES + GRID_W


def _ffn_up(t, u_ref, wg_ref, wv_ref, g_s, val_s, r0, n_lat):
    cols = slice(t * FFN_TF, (t + 1) * FFN_TF)
    wg = wg_ref[:, cols]
    u_main = u_ref[0, pl.ds(r0, FFN_ROWS), :]
    g_s[t, FFN_HALO:FFN_HALO + FFN_ROWS, :] = jnp.dot(u_main, wg, preferred_element_type=F32)
    top0 = pl.multiple_of(jnp.maximum(r0 - GRID_W, 0), GRID_W)
    bot0 = pl.multiple_of(jnp.minimum(r0 + FFN_ROWS, n_lat - GRID_W), GRID_W)
    top = jnp.dot(u_ref[0, pl.ds(top0, GRID_W), :], wg, preferred_element_type=F32)
    bot = jnp.dot(u_ref[0, pl.ds(bot0, GRID_W), :], wg, preferred_element_type=F32)
    g_s[t, 0:SUBLANES, :] = jnp.zeros((SUBLANES, FFN_TF), F32)
    g_s[t, SUBLANES:FFN_HALO, :] = jnp.where(r0 > 0, top, 0.0)
    g_s[t, FFN_HALO + FFN_ROWS:FFN_HALO + FFN_ROWS + GRID_W, :] = jnp.where(r0 + FFN_ROWS < n_lat, bot, 0.0)
    g_s[t, FFN_HALO + FFN_ROWS + GRID_W:, :] = jnp.zeros((SUBLANES, FFN_TF), F32)
    val_s[t] = jnp.dot(u_main, wv_ref[:, cols], preferred_element_type=F32)


def _ffn_conv_act(t, dw_ref, db_ref, g_s, val_s, f_s):
    cols = slice(t * FFN_TF, (t + 1) * FFN_TF)
    dw = dw_ref[:, cols]
    bias = db_ref[:, cols]
    ext = FFN_SUB + 2 * FFN_HALO
    col = (lax.broadcasted_iota(jnp.int32, (ext, 1), 0) + (GRID_W - SUBLANES)) % GRID_W
    for i in range(FFN_ROWS // FFN_SUB):
        base = i * FFN_SUB
        win = g_s[t, base:base + ext, :]
        left = jnp.where(col != 0, pltpu.roll(win, 1, 0), 0.0)
        right = jnp.where(col != GRID_W - 1, pltpu.roll(win, ext - 1, 0), 0.0)
        acc = jnp.zeros((FFN_SUB, FFN_TF), F32) + bias
        for dr in range(3):
            rows = slice(SUBLANES + dr * GRID_W, SUBLANES + dr * GRID_W + FFN_SUB)
            acc += left[rows] * dw[3 * dr:3 * dr + 1]
            acc += win[rows] * dw[3 * dr + 1:3 * dr + 2]
            acc += right[rows] * dw[3 * dr + 2:3 * dr + 3]
        out_rows = slice(base, base + FFN_SUB)
        f_s[t, out_rows, :] = (_gelu_tanh(acc) * val_s[t, out_rows, :]).astype(BF16)


def _ffn_kernel(u_ref, wg_ref, wv_ref, dw_ref, db_ref, wd_ref, h_ref, gate_ref, gpost_ref, o_ref,
                g_s, val_s, f_s, acc_s):
    n_lat = u_ref.shape[1]
    f_idx = pl.program_id(2)
    r0 = pl.multiple_of(pl.program_id(1) * FFN_ROWS, FFN_ROWS)

    @pl.when(f_idx == 0)
    def _():
        acc_s[...] = jnp.zeros_like(acc_s)

    for t in range(FFN_TILES):
        _ffn_up(t, u_ref, wg_ref, wv_ref, g_s, val_s, r0, n_lat)
    for t in range(FFN_TILES):
        _ffn_conv_act(t, dw_ref, db_ref, g_s, val_s, f_s)
    down = [jnp.dot(f_s[t], wd_ref[t * FFN_TF:(t + 1) * FFN_TF, :], preferred_element_type=F32)
            for t in range(FFN_TILES)]
    acc_s[...] += sum(down[1:], down[0])

    @pl.when(f_idx == pl.num_programs(2) - 1)
    def _():
        o_ref[0] = h_ref[0] + _rmsnorm(acc_s[...], gpost_ref[...]) * gate_ref[0]


def _ffn(u2, h1, mod3, w_upg, w_upv, dw9, dwb, w_down, g_post):
    b_sz, n_lat, d = h1.shape
    tf = FFN_TF * FFN_TILES
    return pl.pallas_call(
        _ffn_kernel,
        grid=(b_sz, n_lat // FFN_ROWS, D_FF // tf),
        in_specs=[pl.BlockSpec((1, n_lat, d), lambda b, r, f: (b, 0, 0)),
                  pl.BlockSpec((d, tf), lambda b, r, f: (0, f)),
                  pl.BlockSpec((d, tf), lambda b, r, f: (0, f)),
                  pl.BlockSpec((9, tf), lambda b, r, f: (0, f)),
                  pl.BlockSpec((1, tf), lambda b, r, f: (0, f)),
                  pl.BlockSpec((tf, d), lambda b, r, f: (f, 0)),
                  pl.BlockSpec((1, FFN_ROWS, d), lambda b, r, f: (b, r, 0)),
                  pl.BlockSpec((1, 1, d), lambda b, r, f: (b, 0, 5)),
                  pl.BlockSpec((1, d), lambda b, r, f: (0, 0))],
        out_specs=pl.BlockSpec((1, FFN_ROWS, d), lambda b, r, f: (b, r, 0)),
        out_shape=jax.ShapeDtypeStruct((b_sz, n_lat, d), F32),
        scratch_shapes=[pltpu.VMEM((FFN_TILES, FFN_ROWS + 2 * FFN_HALO, FFN_TF), F32),
                        pltpu.VMEM((FFN_TILES, FFN_ROWS, FFN_TF), F32),
                        pltpu.VMEM((FFN_TILES, FFN_ROWS, FFN_TF), BF16),
                        pltpu.VMEM((FFN_ROWS, d), F32)],
        compiler_params=pltpu.CompilerParams(dimension_semantics=("arbitrary", "arbitrary", "arbitrary"),
                                             vmem_limit_bytes=VMEM_LIMIT),
        name="ffn",
    )(u2, w_upg, w_upv, dw9, dwb, w_down, h1, mod3, g_post)


def _block_diag_groups(w):
    per = LRU_GROUP // LRU_BLOCK_DIM
    w4 = w.reshape(N_LRU_GROUPS, per, LRU_BLOCK_DIM, LRU_BLOCK_DIM)
    eye = jnp.eye(per, dtype=w.dtype)
    return jnp.einsum('gbij,bc->gbicj', w4, eye).reshape(N_LRU_GROUPS, LRU_GROUP, LRU_GROUP)


def kernel(x, c, ctx, c_ctx, w_ada, b_ada, g_pre_mix, g_post_mix, g_pre_ffn, g_post_ffn, w_in, b_merge, dn_conv, dn_a_log, dn_dt_bias, dn_onorm, lru_conv, lru_conv_b, lru_w_rg, lru_b_rg, lru_w_ig, lru_b_ig, lru_lambda, w_branch_dn, w_branch_lru, w_out, w_up, ffn_dw, ffn_dw_b, w_down):
    b_sz, n_lat, d = x.shape
    n_ctx = ctx.shape[1]
    t_all = n_ctx + n_lat
    n_chunks = t_all // CHUNK
    assert w_ada.shape[0] == 1, "single trunk layer"
    assert d == D_MODEL and n_ctx % MERGE_ROWS == 0 and n_lat % FFN_ROWS == 0 and n_ctx % CHUNK == 0

    pad = (-(b_sz + 1)) % SUBLANES
    cc = jnp.concatenate([c, c_ctx[None], jnp.zeros((pad, d), F32)], axis=0)
    mod = _ada(cc, w_ada[0], b_ada)
    mod3 = mod.reshape(mod.shape[0], 1, 6 * d)

    wl = w_in[0]
    ab0 = 4 * DN_WIDTH
    w_main = jnp.concatenate([wl[:, :ab0], wl[:, ab0 + N_AB:]], axis=1).astype(BF16)
    w_ab = jnp.pad(wl[:, ab0:ab0 + N_AB], ((0, 0), (0, LANES - N_AB))).astype(BF16)
    half_ab = N_DIRS * DN_HEADS
    alog = jnp.pad(dn_a_log[0].reshape(1, half_ab), ((0, 0), (0, LANES - half_ab)))
    dtb = jnp.pad(dn_dt_bias[0].reshape(1, half_ab), ((0, 0), (0, LANES - half_ab)))

    proj, gates, gates_t = _inproj(ctx, x, mod3, g_pre_mix, w_main, w_ab, alog, dtb)
    grow = gates_t.reshape(b_sz, N_DIRS * DN_HEADS, n_chunks, CHUNK)
    y_dn = _deltanet(proj, gates, grow, dn_conv[0], dn_onorm, n_ctx)

    w_gates = jnp.concatenate([_block_diag_groups(lru_w_rg[0, 0]), _block_diag_groups(lru_w_ig[0, 0]),
                               _block_diag_groups(lru_w_rg[0, 1]), _block_diag_groups(lru_w_ig[0, 1])],
                              axis=-1).astype(BF16)
    grp = lambda v: v.reshape(N_LRU_GROUPS, 1, LRU_GROUP)
    b_gates = jnp.concatenate([grp(lru_b_rg[0, 0]), grp(lru_b_ig[0, 0]),
                               grp(lru_b_rg[0, 1]), grp(lru_b_ig[0, 1])], axis=-1)
    lam = jnp.concatenate([grp(lru_lambda[0, 0]), grp(lru_lambda[0, 1])], axis=-1)
    y_lru = _rglru(proj, lru_conv[0], lru_conv_b, w_gates, b_gates, lam, n_ctx)

    h1, u2 = _merge(y_dn, y_lru, proj, x, mod3, b_merge, w_branch_dn[0].astype(BF16),
                    w_branch_lru[0].astype(BF16), w_out[0].astype(BF16), g_post_mix, g_pre_ffn, n_ctx)

    w_up_l = w_up[0]
    return _ffn(u2, h1, mod3, w_up_l[:, :D_FF].astype(BF16), w_up_l[:, D_FF:].astype(BF16),
                ffn_dw[0].reshape(9, D_FF), ffn_dw_b, w_down[0].astype(BF16), g_post_ffn)
```

```python
import functools

import jax
import jax.numpy as jnp
from jax import lax
from jax.experimental import pallas as pl
from jax.experimental.pallas import tpu as pltpu

F32 = jnp.float32
BF16 = jnp.bfloat16

EPS = 1e-6
D_MODEL = 1024
GRID_W = 64
DN_HEADS = 8
DN_HEAD_DIM = 128
DN_WIDTH = DN_HEADS * DN_HEAD_DIM
CHUNK = 64
LRU_WIDTH = 1024
LRU_BLOCKS = 16
LRU_BLOCK_DIM = LRU_WIDTH // LRU_BLOCKS
LRU_C = 8.0
N_DIRS = 2
D_FF = 4 * D_MODEL
N_AB = 2 * N_DIRS * DN_HEADS
D_MAIN = 4 * DN_WIDTH + 2 * LRU_WIDTH + 2 * D_MODEL

LANES = 128
SUBLANES = 8
BF16_ROWS = 16
LRU_GROUP = 256
N_LRU_GROUPS = LRU_WIDTH // LRU_GROUP
VMEM_LIMIT = 56 * 1024 * 1024


def _bdot(a, b):
    return jnp.dot(a.astype(BF16), b.astype(BF16), preferred_element_type=F32)


def _split(a):
    hi = a.astype(BF16)
    return hi, (a - hi.astype(F32)).astype(BF16)


def _dot3(a, b):
    a_hi, a_lo = _split(a)
    b_hi, b_lo = _split(b)
    d = functools.partial(jnp.dot, preferred_element_type=F32)
    return d(a_hi, b_hi) + (d(a_hi, b_lo) + d(a_lo, b_hi))


def _silu(x):
    return x * jax.nn.sigmoid(x)


def _softplus(x):
    return jnp.maximum(x, 0.0) + jnp.log1p(jnp.exp(-jnp.abs(x)))


def _gelu_tanh(x):
    c1 = 2.0 * 0.7978845608028654 * 1.4426950408889634
    c2 = c1 * 0.044715
    return x / (1.0 + jnp.exp2(-(x * (c1 + c2 * (x * x)))))


def _rmsnorm(v, gain):
    ms = jnp.mean(v * v, axis=-1, keepdims=True)
    return v * lax.rsqrt(ms + EPS) * gain


def _seg_pos(t_all, n_ctx):
    row = lax.broadcasted_iota(jnp.int32, (t_all, 1), 0)
    in_ctx = row < n_ctx
    return jnp.where(in_ctx, row, row - n_ctx), jnp.where(in_ctx, n_ctx, t_all - n_ctx)


def _seg_conv4(x, w, pos, seglen):
    t_all = x.shape[0]
    acc = x * w[2:3]
    acc += jnp.where(pos >= 2, pltpu.roll(x, 2, 0), 0.0) * w[0:1]
    acc += jnp.where(pos >= 1, pltpu.roll(x, 1, 0), 0.0) * w[1:2]
    acc += jnp.where(pos <= seglen - 2, pltpu.roll(x, t_all - 1, 0), 0.0) * w[3:4]
    return acc


def _ada_kernel(c_ref, w_ref, b_ref, o_ref):
    o_ref[...] = _dot3(_silu(c_ref[...]), w_ref[...]) + b_ref[...]


def _ada(cc, w_ada, b_ada):
    rows, d = cc.shape
    n = w_ada.shape[1]
    tn = 1536
    return pl.pallas_call(
        _ada_kernel,
        grid=(n // tn,),
        in_specs=[pl.BlockSpec((rows, d), lambda j: (0, 0)),
                  pl.BlockSpec((d, tn), lambda j: (0, j)),
                  pl.BlockSpec((1, tn), lambda j: (0, j))],
        out_specs=pl.BlockSpec((rows, tn), lambda j: (0, j)),
        out_shape=jax.ShapeDtypeStruct((rows, n), F32),
        compiler_params=pltpu.CompilerParams(dimension_semantics=("arbitrary",),
                                             vmem_limit_bytes=VMEM_LIMIT),
        name="ada",
    )(cc, w_ada, b_ada)


def _chunk_cumsum(g, reverse):
    t_all = g.shape[0]
    pos = lax.broadcasted_iota(jnp.int32, (t_all, 1), 0) % CHUNK
    s = 1
    while s < CHUNK:
        if reverse:
            g = g + jnp.where(pos < CHUNK - s, pltpu.roll(g, t_all - s, 0), 0.0)
        else:
            g = g + jnp.where(pos >= s, pltpu.roll(g, s, 0), 0.0)
        s *= 2
    return g


def _inproj_kernel(n_ctx, ctx_ref, x_ref, shc_ref, scc_ref, shx_ref, scx_ref, g_ref, w_ref, wab_ref,
                   alog_ref, dtb_ref, o_ref, gates_ref, gates_t_ref, u_ref):
    @pl.when(pl.program_id(1) == 0)
    def _():
        gain = g_ref[...]
        u_ref[0:n_ctx] = (_rmsnorm(ctx_ref[0], gain) * (1.0 + scc_ref[0]) + shc_ref[0]).astype(BF16)
        u_ref[n_ctx:] = (_rmsnorm(x_ref[0], gain) * (1.0 + scx_ref[0]) + shx_ref[0]).astype(BF16)
        ab = jnp.dot(u_ref[...], wab_ref[...], preferred_element_type=F32)
        g = -jnp.exp(alog_ref[...]) * _softplus(ab + dtb_ref[...])
        lane = lax.broadcasted_iota(jnp.int32, (1, LANES), 1)
        gc = jnp.where(lane < DN_HEADS, _chunk_cumsum(g, False), _chunk_cumsum(g, True))
        gates = jnp.where(lane < N_DIRS * DN_HEADS, gc, jax.nn.sigmoid(ab))
        gates_ref[0] = gates
        gates_t_ref[0] = gates.T[0:N_DIRS * DN_HEADS]

    o_ref[0] = jnp.dot(u_ref[...], w_ref[...], preferred_element_type=F32).astype(BF16)


def _inproj(ctx, x, mod3, g_pre, w_main, w_ab, alog, dtb):
    b_sz, n_ctx, d = ctx.shape
    n_lat = x.shape[1]
    t_all = n_ctx + n_lat
    tn = 1024
    c_row = b_sz
    vec = lambda k, ctx_row: pl.BlockSpec(
        (1, 1, d), (lambda b, j: (c_row, 0, k)) if ctx_row else (lambda b, j: (b, 0, k)))
    return pl.pallas_call(
        functools.partial(_inproj_kernel, n_ctx),
        grid=(b_sz, D_MAIN // tn),
        in_specs=[pl.BlockSpec((1, n_ctx, d), lambda b, j: (b, 0, 0)),
                  pl.BlockSpec((1, n_lat, d), lambda b, j: (b, 0, 0)),
                  vec(0, True), vec(1, True), vec(0, False), vec(1, False),
                  pl.BlockSpec((1, d), lambda b, j: (0, 0)),
                  pl.BlockSpec((d, tn), lambda b, j: (0, j)),
                  pl.BlockSpec((d, LANES), lambda b, j: (0, 0)),
                  pl.BlockSpec((1, LANES), lambda b, j: (0, 0)),
                  pl.BlockSpec((1, LANES), lambda b, j: (0, 0))],
        out_specs=[pl.BlockSpec((1, t_all, tn), lambda b, j: (b, 0, j)),
                   pl.BlockSpec((1, t_all, LANES), lambda b, j: (b, 0, 0)),
                   pl.BlockSpec((1, N_DIRS * DN_HEADS, t_all), lambda b, j: (b, 0, 0))],
        out_shape=[jax.ShapeDtypeStruct((b_sz, t_all, D_MAIN), BF16),
                   jax.ShapeDtypeStruct((b_sz, t_all, LANES), F32),
                   jax.ShapeDtypeStruct((b_sz, N_DIRS * DN_HEADS, t_all), F32)],
        scratch_shapes=[pltpu.VMEM((t_all, d), BF16)],
        compiler_params=pltpu.CompilerParams(dimension_semantics=("arbitrary", "arbitrary"),
                                             vmem_limit_bytes=VMEM_LIMIT),
        name="inproj",
    )(ctx, x, mod3, mod3, mod3, mod3, g_pre, w_main, w_ab, alog, dtb)


DN_GROUP = 12


def _dn_group_terms(chains, q_s, k_s, v_s):
    dk = DN_HEAD_DIM
    ii = lax.broadcasted_iota(jnp.int32, (CHUNK, CHUNK), 0)
    jj = lax.broadcasted_iota(jnp.int32, (CHUNK, CHUNK), 1)
    eye = (ii == jj).astype(F32)
    nt = (((1,), (1,)), ((), ()))
    tn = (((0,), (0,)), ((), ()))

    scores = []
    for d, rows, gc, beta, gr in chains:
        k = k_s[rows, :]
        kq = jnp.concatenate([k * beta, q_s[rows, :]], axis=0).astype(BF16)
        scores.append(lax.dot_general(kq, k.astype(BF16), nt, preferred_element_type=F32))
    yield None

    low, attn = [], []
    for (d, rows, gc, beta, gr), s in zip(chains, scores):
        incl = (ii >= jj) if d == 0 else (ii <= jj)
        strict = (ii > jj) if d == 0 else (ii < jj)
        decay = jnp.where(incl, jnp.exp(jnp.minimum(gc - gr, 0.0)), 0.0)
        low.append(jnp.where(strict, s[:CHUNK] * decay, 0.0))
        attn.append((s[CHUNK:] * decay).astype(BF16))

    def joins(m):
        return ((ii // (2 * m)) == (jj // (2 * m))) & ((ii // m) != (jj // m))

    inv = [eye - jnp.where(joins(1), l, 0.0) for l in low]
    m = 2
    while m < CHUNK:
        mask = joins(m)
        half = [_bdot(t, jnp.where(mask, l, 0.0)) for t, l in zip(inv, low)]
        yield None
        inv = [t - _bdot(hf, t) for t, hf in zip(inv, half)]
        yield None
        m *= 2

    wu = []
    for (d, rows, gc, beta, gr), t in zip(chains, inv):
        kb = k_s[rows, :] * beta
        rhs = jnp.concatenate([kb * jnp.exp(gc), v_s[rows, :] * beta], axis=1)
        wu.append(_bdot(t, rhs).astype(BF16))
    yield None

    au = [jnp.dot(a, w, preferred_element_type=F32) for a, w in zip(attn, wu)]
    yield None
    ku = []
    for (d, rows, gc, beta, gr), w in zip(chains, wu):
        g_last = gc[CHUNK - 1:CHUNK] if d == 0 else gc[0:1]
        kg = (k_s[rows, :] * jnp.exp(g_last - gc)).astype(BF16)
        ku.append(lax.dot_general(kg, w, tn, preferred_element_type=F32))
    yield None

    out = []
    for (d, rows, gc, beta, gr), a, kk in zip(chains, au, ku):
        qe = q_s[rows, :] * jnp.exp(gc) - a[:, :dk]
        out.append((-kk[:, :dk], qe, kk[:, dk:], a[:, dk:]))
    yield out


def _dn_kernel(n_ctx, q_ref, k_ref, v_ref, z_ref, gates_ref, grow_ref, wq_ref, wk_ref, wv_ref, on_ref,
               o_ref, q_s, k_s, v_s, gcol_s, mq_s, add_s, o0_s, out_s):
    t_all = q_ref.shape[1]
    n_chunks = t_all // CHUNK
    ctx_chunks = n_ctx // CHUNK
    n_trips = n_chunks // DN_GROUP
    dk = DN_HEAD_DIM
    head = pl.program_id(1)
    gate_col = lambda rows, kk: gcol_s[rows, kk * DN_HEADS:kk * DN_HEADS + 1]

    def conv_silu(ref, w_ref, lo_out, n_out):
        lo, hi = max(lo_out - BF16_ROWS, 0), min(lo_out + n_out + BF16_ROWS, t_all)
        m = hi - lo
        x = ref[0, lo:hi, :].astype(F32)
        w = w_ref[...]
        touches = lambda edge: lo <= edge + 2 and hi >= edge - 2
        if touches(0) or touches(n_ctx) or touches(t_all):
            row = lax.broadcasted_iota(jnp.int32, (m, 1), 0) + lo
            in_ctx = row < n_ctx
            p = jnp.where(in_ctx, row, row - n_ctx)
            last = jnp.where(in_ctx, n_ctx, t_all - n_ctx) - 1
            tap = lambda shift, ok: jnp.where(ok, pltpu.roll(x, shift, 0), 0.0)
            taps = (tap(2, p >= 2), tap(1, p >= 1), tap(m - 1, p < last))
        else:
            taps = (pltpu.roll(x, 2, 0), pltpu.roll(x, 1, 0), pltpu.roll(x, m - 1, 0))
        acc = x * w[2:3] + taps[0] * w[0:1] + taps[1] * w[1:2] + taps[2] * w[3:4]
        return _silu(acc)[lo_out - lo:lo_out - lo + n_out]

    def l2n(t):
        return t * lax.rsqrt(jnp.sum(t * t, axis=-1, keepdims=True) + EPS)

    def prepare(lo, n):
        rows = slice(lo, lo + n)
        q_s[rows, :] = l2n(conv_silu(q_ref, wq_ref, lo, n)) * (DN_HEAD_DIM ** -0.5)
        k_s[rows, :] = l2n(conv_silu(k_ref, wk_ref, lo, n))
        v_s[rows, :] = conv_silu(v_ref, wv_ref, lo, n)
        gcol_s[rows, :] = pltpu.roll(gates_ref[0, rows, :], (LANES - head) % LANES, 1)

    first_f = DN_GROUP * CHUNK
    first_b = (n_chunks - (DN_GROUP - ctx_chunks)) * CHUNK
    blocks = [(0, first_f // 2), (first_f // 2, first_f // 2), (first_b, t_all - first_b)]
    n_mid = 4
    mid = (first_b - first_f) // n_mid
    blocks += [(first_f + i * mid, mid) for i in range(n_mid)]
    for lo, n in blocks:
        prepare(lo, n)

    def chunk_at(d, s):
        if d == 0:
            return s
        if isinstance(s, int):
            return ctx_chunks - 1 - s if s < ctx_chunks else n_chunks + ctx_chunks - 1 - s
        return jnp.where(s < ctx_chunks, ctx_chunks - 1 - s, n_chunks + ctx_chunks - 1 - s)

    def chunk_rows(c):
        start = c * CHUNK
        return pl.ds(start if isinstance(c, int) else pl.multiple_of(start, CHUNK), CHUNK)

    def seq_step(s, states):
        new_states = []
        for d in range(N_DIRS):
            c = chunk_at(d, s)
            rows = chunk_rows(c)
            edge = c * CHUNK + (CHUNK - 1 if d == 0 else 0)
            g_last = gate_col(pl.ds(edge, 1), d)
            r = jnp.dot(mq_s[d, c], states[d].astype(BF16), preferred_element_type=F32)
            out_s[d, rows, :] = r[dk:] + o0_s[d, rows, :]
            new_states.append(states[d] * jnp.exp(g_last) + (r[:dk] + add_s[d, c]))
        return tuple(new_states)

    def trip(g_pre, g_seq, states):
        pending = [] if g_seq is None else [g_seq * DN_GROUP + j for j in range(DN_GROUP)]
        if g_pre is not None:
            chains = []
            for j in range(DN_GROUP):
                s = g_pre * DN_GROUP + j
                for d in range(N_DIRS):
                    c = chunk_at(d, s)
                    rows = chunk_rows(c)
                    chains.append((d, rows, gate_col(rows, d), gate_col(rows, N_DIRS + d),
                                   grow_ref[0, d * DN_HEADS + head, pl.ds(c, 1), :], c))
            terms = None
            for terms in _dn_group_terms([ch[:5] for ch in chains], q_s, k_s, v_s):
                if terms is None and pending:
                    states = seq_step(pending.pop(0), states)
            for (d, rows, _, _, _, c), (trans, qe, add, o0) in zip(chains, terms):
                mq_s[d, c, 0:dk, :] = trans.astype(BF16)
                mq_s[d, c, dk:, :] = qe.astype(BF16)
                add_s[d, c] = add
                o0_s[d, rows, :] = o0
        for s in pending:
            states = seq_step(s, states)
        return states

    zero = jnp.zeros((dk, dk), F32)
    states = trip(0, None, (zero, zero))
    states = lax.fori_loop(1, n_trips, lambda g, st: trip(g, g - 1, st), states)

    def finish(lo, hi):
        rows = slice(lo * CHUNK, hi * CHUNK)
        o = out_s[0, rows, :] + out_s[1, rows, :]
        y = _rmsnorm(o, on_ref[...]) * _silu(z_ref[0, rows, :].astype(F32))
        o_ref[0, lo * CHUNK - n_ctx:hi * CHUNK - n_ctx, :] = y.astype(BF16)

    done_lo = n_chunks + ctx_chunks - (n_trips - 1) * DN_GROUP
    done_hi = (n_trips - 1) * DN_GROUP
    finish(done_lo, done_hi)
    trip(None, n_trips - 1, states)
    finish(ctx_chunks, done_lo)
    finish(done_hi, n_chunks)


def _deltanet(proj, gates, grow, dn_conv, onorm, n_ctx):
    b_sz, t_all, _ = proj.shape
    n_lat = t_all - n_ctx
    n_chunks = t_all // CHUNK
    hd = DN_HEAD_DIM
    col = lambda part: pl.BlockSpec((1, t_all, hd), lambda b, h: (b, 0, part * DN_HEADS + h))
    wcol = lambda part: pl.BlockSpec((dn_conv.shape[0], hd), lambda b, h: (0, part * DN_HEADS + h))
    return pl.pallas_call(
        functools.partial(_dn_kernel, n_ctx),
        grid=(b_sz, DN_HEADS),
        in_specs=[col(0), col(1), col(2), col(3),
                  pl.BlockSpec((1, t_all, LANES), lambda b, h: (b, 0, 0)),
                  pl.BlockSpec((1, N_DIRS * DN_HEADS, n_chunks, CHUNK), lambda b, h: (b, 0, 0, 0)),
                  wcol(0), wcol(1), wcol(2),
                  pl.BlockSpec((1, hd), lambda b, h: (0, 0))],
        out_specs=pl.BlockSpec((1, n_lat, hd), lambda b, h: (b, 0, h)),
        out_shape=jax.ShapeDtypeStruct((b_sz, n_lat, DN_WIDTH), BF16),
        scratch_shapes=[pltpu.VMEM((t_all, hd), F32), pltpu.VMEM((t_all, hd), F32),
                        pltpu.VMEM((t_all, hd), F32),
                        pltpu.VMEM((t_all, LANES), F32),
                        pltpu.VMEM((N_DIRS, n_chunks, hd + CHUNK, hd), BF16),
                        pltpu.VMEM((N_DIRS, n_chunks, hd, hd), F32),
                        pltpu.VMEM((N_DIRS, t_all, hd), F32),
                        pltpu.VMEM((N_DIRS, t_all, hd), F32)],
        compiler_params=pltpu.CompilerParams(dimension_semantics=("arbitrary", "arbitrary"),
                                             vmem_limit_bytes=VMEM_LIMIT),
        name="deltanet",
    )(proj, proj, proj, proj, gates, grow, dn_conv, dn_conv, dn_conv, onorm)


LRU_ROW_TILE = 256


def _lru_kernel(n_ctx, xl_ref, yl_ref, wc_ref, bc_ref, wg_ref, bg_ref, lam_ref, o_ref,
                xc_s, a_s, b_s, h_s):
    t_all = xl_ref.shape[1]
    gw = LRU_GROUP
    pos, seglen = _seg_pos(t_all, n_ctx)
    xc_s[...] = _seg_conv4(xl_ref[0].astype(F32), wc_ref[...], pos, seglen) + bc_ref[...]
    sp = _softplus(-lam_ref[0])
    n_half = gw // LANES

    def gates(i, carry):
        rows = pl.ds(pl.multiple_of(i * LRU_ROW_TILE, LRU_ROW_TILE), LRU_ROW_TILE)
        xc = xc_s[rows, :]
        y = jnp.dot(xc.astype(BF16), wg_ref[0], preferred_element_type=F32) + bg_ref[0]
        for d in range(N_DIRS):
            r = jax.nn.sigmoid(y[:, (2 * d) * gw:(2 * d + 1) * gw])
            ig = jax.nn.sigmoid(y[:, (2 * d + 1) * gw:(2 * d + 2) * gw])
            log_a = (-LRU_C) * r * sp[:, d * gw:(d + 1) * gw]
            a = jnp.exp(log_a)
            b = jnp.sqrt(-jnp.tanh(log_a) * (a * a + 1.0)) * (ig * xc)
            for hl in range(n_half):
                a_s[d * n_half + hl, rows, :] = a[:, hl * LANES:(hl + 1) * LANES]
                b_s[d * n_half + hl, rows, :] = b[:, hl * LANES:(hl + 1) * LANES]
        n_sub = LRU_ROW_TILE // SUBLANES
        views = [pl.ds(i * LRU_ROW_TILE + k, n_sub, stride=SUBLANES) for k in range(SUBLANES)]
        for dh in range(N_DIRS * n_half):
            order = list(range(SUBLANES)) if dh < n_half else list(range(SUBLANES - 1, -1, -1))
            loaded = {k: (a_s[dh, views[k], :], b_s[dh, views[k], :]) for k in order}
            a_run, b_run = loaded[order[0]]
            scanned = {}
            for k in order[1:]:
                a_k, b_k = loaded[k]
                b_run = a_k * b_run + b_k
                a_run = a_k * a_run
                scanned[k] = (a_run, b_run)
            for k, (a_k, b_k) in scanned.items():
                a_s[dh, views[k], :] = a_k
                b_s[dh, views[k], :] = b_k
        return carry

    lax.fori_loop(0, t_all // LRU_ROW_TILE, gates, 0)

    n_tiles = t_all // SUBLANES
    ctx_tiles = n_ctx // SUBLANES

    def carry_step(s, carries):
        tile_of = (s, jnp.where(s < ctx_tiles, ctx_tiles - 1 - s, n_tiles + ctx_tiles - 1 - s))
        new = []
        for d in range(N_DIRS):
            rows = pl.ds(pl.multiple_of(tile_of[d] * SUBLANES, SUBLANES), SUBLANES)
            wide = lambda ref: jnp.concatenate([ref[d * n_half + hl, rows, :] for hl in range(n_half)], axis=1)
            h = wide(b_s) + wide(a_s) * carries[d]
            h_s[d, rows, :] = h
            edge = h[SUBLANES - 1:SUBLANES] if d == 0 else h[0:1]
            new.append(jnp.broadcast_to(edge, h.shape))
        return tuple(new)

    zero = jnp.zeros((SUBLANES, gw), F32)
    lax.fori_loop(0, n_tiles, carry_step, (zero, zero), unroll=8)
    h = h_s[0, n_ctx:, :] + h_s[1, n_ctx:, :]
    o_ref[0] = (h * _gelu_tanh(yl_ref[0, n_ctx:, :].astype(F32))).astype(BF16)


def _rglru(proj, lru_conv, lru_conv_b, w_gates, b_gates, lam, n_ctx):
    b_sz, t_all, _ = proj.shape
    n_lat = t_all - n_ctx
    gw = LRU_GROUP
    x_blk = 4 * DN_WIDTH // gw
    y_blk = (4 * DN_WIDTH + LRU_WIDTH) // gw
    return pl.pallas_call(
        functools.partial(_lru_kernel, n_ctx),
        grid=(b_sz, N_LRU_GROUPS),
        in_specs=[pl.BlockSpec((1, t_all, gw), lambda b, g: (b, 0, x_blk + g)),
                  pl.BlockSpec((1, t_all, gw), lambda b, g: (b, 0, y_blk + g)),
                  pl.BlockSpec((lru_conv.shape[0], gw), lambda b, g: (0, g)),
                  pl.BlockSpec((1, gw), lambda b, g: (0, g)),
                  pl.BlockSpec((1, gw, 2 * N_DIRS * gw), lambda b, g: (g, 0, 0)),
                  pl.BlockSpec((1, 1, 2 * N_DIRS * gw), lambda b, g: (g, 0, 0)),
                  pl.BlockSpec((1, 1, N_DIRS * gw), lambda b, g: (g, 0, 0))],
        out_specs=pl.BlockSpec((1, n_lat, gw), lambda b, g: (b, 0, g)),
        out_shape=jax.ShapeDtypeStruct((b_sz, n_lat, LRU_WIDTH), BF16),
        scratch_shapes=[pltpu.VMEM((t_all, gw), F32),
                        pltpu.VMEM((N_DIRS * (gw // LANES), t_all, LANES), F32),
                        pltpu.VMEM((N_DIRS * (gw // LANES), t_all, LANES), F32),
                        pltpu.VMEM((N_DIRS, t_all, gw), F32)],
        compiler_params=pltpu.CompilerParams(dimension_semantics=("arbitrary", "arbitrary"),
                                             vmem_limit_bytes=VMEM_LIMIT),
        name="rglru",
    )(proj, proj, lru_conv, lru_conv_b, w_gates, b_gates, lam)


MERGE_ROWS = 256


def _merge_kernel(ydn_ref, ylru_ref, mg_ref, x_ref, bm_ref, wdn_ref, wlru_ref, wout_ref, gpost_ref,
                  gate_ref, gpre_ref, sh_ref, sc_ref, h_ref, u_ref):
    d = D_MODEL
    gl = jax.nn.sigmoid(mg_ref[0].astype(F32) + bm_ref[...])
    p_dn = jnp.dot(ydn_ref[0], wdn_ref[...], preferred_element_type=F32)
    p_lru = jnp.dot(ylru_ref[0], wlru_ref[...], preferred_element_type=F32)
    mix = _bdot(gl[:, :d] * p_dn + gl[:, d:] * p_lru, wout_ref[...])
    h = x_ref[0] + _rmsnorm(mix, gpost_ref[...]) * gate_ref[0]
    h_ref[0] = h
    u_ref[0] = (_rmsnorm(h, gpre_ref[...]) * (1.0 + sc_ref[0]) + sh_ref[0]).astype(BF16)


def _merge(y_dn, y_lru, proj, x, mod3, b_merge, w_dn, w_lru, w_out, g_post, g_pre_ffn, n_ctx):
    b_sz, n_lat, d = x.shape
    tm = MERGE_ROWS
    row0 = n_ctx // tm
    mg_blk = (4 * DN_WIDTH + 2 * LRU_WIDTH) // (2 * d)
    tile = lambda: pl.BlockSpec((1, tm, d), lambda b, i: (b, i, 0))
    full = lambda r, c: pl.BlockSpec((r, c), lambda b, i: (0, 0))
    vec = lambda k: pl.BlockSpec((1, 1, d), lambda b, i: (b, 0, k))
    return pl.pallas_call(
        _merge_kernel,
        grid=(b_sz, n_lat // tm),
        in_specs=[tile(), tile(),
                  pl.BlockSpec((1, tm, 2 * d), lambda b, i: (b, row0 + i, mg_blk)),
                  tile(), full(1, 2 * d), full(d, d), full(d, d), full(d, d), full(1, d),
                  vec(2), full(1, d), vec(3), vec(4)],
        out_specs=[tile(), tile()],
        out_shape=[jax.ShapeDtypeStruct((b_sz, n_lat, d), F32),
                   jax.ShapeDtypeStruct((b_sz, n_lat, d), BF16)],
        compiler_params=pltpu.CompilerParams(dimension_semantics=("arbitrary", "arbitrary"),
                                             vmem_limit_bytes=VMEM_LIMIT),
        name="merge",
    )(y_dn, y_lru, proj, x, b_merge, w_dn, w_lru, w_out, g_post, mod3, g_pre_ffn, mod3, mod3)


FFN_ROWS = 1024
FFN_TF = 512
FFN_TILES = 2
FFN_SUB = 256
FFN_HALO = SUBLANES + GRID_W


def _ffn_up(t, u_ref, wg_ref, wv_ref, g_s, val_s, r0, n_lat):
    cols = slice(t * FFN_TF, (t + 1) * FFN_TF)
    wg = wg_ref[:, cols]
    u_main = u_ref[0, pl.ds(r0, FFN_ROWS), :]
    g_s[t, FFN_HALO:FFN_HALO + FFN_ROWS, :] = jnp.dot(u_main, wg, preferred_element_type=F32)
    top0 = pl.multiple_of(jnp.maximum(r0 - GRID_W, 0), GRID_W)
    bot0 = pl.multiple_of(jnp.minimum(r0 + FFN_ROWS, n_lat - GRID_W), GRID_W)
    top = jnp.dot(u_ref[0, pl.ds(top0, GRID_W), :], wg, preferred_element_type=F32)
    bot = jnp.dot(u_ref[0, pl.ds(bot0, GRID_W), :], wg, preferred_element_type=F32)
    g_s[t, 0:SUBLANES, :] = jnp.zeros((SUBLANES, FFN_TF), F32)
    g_s[t, SUBLANES:FFN_HALO, :] = jnp.where(r0 > 0, top, 0.0)
    g_s[t, FFN_HALO + FFN_ROWS:FFN_HALO + FFN_ROWS + GRID_W, :] = jnp.where(r0 + FFN_ROWS < n_lat, bot, 0.0)
    g_s[t, FFN_HALO + FFN_ROWS + GRID_W:, :] = jnp.zeros((SUBLANES, FFN_TF), F32)
    val_s[t] = jnp.dot(u_main, wv_ref[:, cols], preferred_element_type=F32)


def _ffn_conv_act(t, dw_ref, db_ref, g_s, val_s, f_s):
    cols = slice(t * FFN_TF, (t + 1) * FFN_TF)
    dw = dw_ref[:, cols]
    bias = db_ref[:, cols]
    ext = FFN_SUB + 2 * SUBLANES
    col = lax.broadcasted_iota(jnp.int32, (FFN_SUB, 1), 0) % GRID_W
    mid = slice(SUBLANES, SUBLANES + FFN_SUB)
    for i in range(FFN_ROWS // FFN_SUB):
        base = i * FFN_SUB
        wins = [g_s[t, base + dr * GRID_W:base + dr * GRID_W + ext, :] for dr in range(3)]
        part = [sum(wins[dr] * dw[3 * dr + dc:3 * dr + dc + 1] for dr in range(3)) for dc in range(3)]
        left = jnp.where(col != 0, pltpu.roll(part[0], 1, 0)[mid], 0.0)
        right = jnp.where(col != GRID_W - 1, pltpu.roll(part[2], ext - 1, 0)[mid], 0.0)
        acc = part[1][mid] + left + right + bias
        out_rows = slice(base, base + FFN_SUB)
        f_s[t, out_rows, :] = (_gelu_tanh(acc) * val_s[t, out_rows, :]).astype(BF16)


def _ffn_kernel(u_ref, wg_ref, wv_ref, dw_ref, db_ref, wd_ref, h_ref, gate_ref, gpost_ref, o_ref,
                g_s, val_s, f_s, acc_s):
    n_lat = u_ref.shape[1]
    f_idx = pl.program_id(2)
    r0 = pl.multiple_of(pl.program_id(1) * FFN_ROWS, FFN_ROWS)

    @pl.when(f_idx == 0)
    def _():
        acc_s[...] = jnp.zeros_like(acc_s)

    for t in range(FFN_TILES):
        _ffn_up(t, u_ref, wg_ref, wv_ref, g_s, val_s, r0, n_lat)
    for t in range(FFN_TILES):
        _ffn_conv_act(t, dw_ref, db_ref, g_s, val_s, f_s)
    down = [jnp.dot(f_s[t], wd_ref[t * FFN_TF:(t + 1) * FFN_TF, :], preferred_element_type=F32)
            for t in range(FFN_TILES)]
    acc_s[...] += sum(down[1:], down[0])

    @pl.when(f_idx == pl.num_programs(2) - 1)
    def _():
        o_ref[0] = h_ref[0] + _rmsnorm(acc_s[...], gpost_ref[...]) * gate_ref[0]


def _ffn(u2, h1, mod3, w_upg, w_upv, dw9, dwb, w_down, g_post):
    b_sz, n_lat, d = h1.shape
    tf = FFN_TF * FFN_TILES
    return pl.pallas_call(
        _ffn_kernel,
        grid=(b_sz, n_lat // FFN_ROWS, D_FF // tf),
        in_specs=[pl.BlockSpec((1, n_lat, d), lambda b, r, f: (b, 0, 0)),
                  pl.BlockSpec((d, tf), lambda b, r, f: (0, f)),
                  pl.BlockSpec((d, tf), lambda b, r, f: (0, f)),
                  pl.BlockSpec((9, tf), lambda b, r, f: (0, f)),
                  pl.BlockSpec((1, tf), lambda b, r, f: (0, f)),
                  pl.BlockSpec((tf, d), lambda b, r, f: (f, 0)),
                  pl.BlockSpec((1, FFN_ROWS, d), lambda b, r, f: (b, r, 0)),
                  pl.BlockSpec((1, 1, d), lambda b, r, f: (b, 0, 5)),
                  pl.BlockSpec((1, d), lambda b, r, f: (0, 0))],
        out_specs=pl.BlockSpec((1, FFN_ROWS, d), lambda b, r, f: (b, r, 0)),
        out_shape=jax.ShapeDtypeStruct((b_sz, n_lat, d), F32),
        scratch_shapes=[pltpu.VMEM((FFN_TILES, FFN_ROWS + 2 * FFN_HALO, FFN_TF), F32),
                        pltpu.VMEM((FFN_TILES, FFN_ROWS, FFN_TF), F32),
                        pltpu.VMEM((FFN_TILES, FFN_ROWS, FFN_TF), BF16),
                        pltpu.VMEM((FFN_ROWS, d), F32)],
        compiler_params=pltpu.CompilerParams(dimension_semantics=("arbitrary", "arbitrary", "arbitrary"),
                                             vmem_limit_bytes=VMEM_LIMIT),
        name="ffn",
    )(u2, w_upg, w_upv, dw9, dwb, w_down, h1, mod3, g_post)


def _block_diag_groups(w):
    per = LRU_GROUP // LRU_BLOCK_DIM
    w4 = w.reshape(N_LRU_GROUPS, per, LRU_BLOCK_DIM, LRU_BLOCK_DIM)
    eye = jnp.eye(per, dtype=w.dtype)
    return jnp.einsum('gbij,bc->gbicj', w4, eye).reshape(N_LRU_GROUPS, LRU_GROUP, LRU_GROUP)


def kernel(x, c, ctx, c_ctx, w_ada, b_ada, g_pre_mix, g_post_mix, g_pre_ffn, g_post_ffn, w_in, b_merge, dn_conv, dn_a_log, dn_dt_bias, dn_onorm, lru_conv, lru_conv_b, lru_w_rg, lru_b_rg, lru_w_ig, lru_b_ig, lru_lambda, w_branch_dn, w_branch_lru, w_out, w_up, ffn_dw, ffn_dw_b, w_down):
    b_sz, n_lat, d = x.shape
    n_ctx = ctx.shape[1]
    t_all = n_ctx + n_lat
    n_chunks = t_all // CHUNK
    assert w_ada.shape[0] == 1, "single trunk layer"
    assert d == D_MODEL and n_ctx % MERGE_ROWS == 0 and n_lat % FFN_ROWS == 0 and n_ctx % CHUNK == 0

    pad = (-(b_sz + 1)) % SUBLANES
    cc = jnp.concatenate([c, c_ctx[None], jnp.zeros((pad, d), F32)], axis=0)
    mod = _ada(cc, w_ada[0], b_ada)
    mod3 = mod.reshape(mod.shape[0], 1, 6 * d)

    wl = w_in[0]
    ab0 = 4 * DN_WIDTH
    w_main = jnp.concatenate([wl[:, :ab0], wl[:, ab0 + N_AB:]], axis=1).astype(BF16)
    w_ab = jnp.pad(wl[:, ab0:ab0 + N_AB], ((0, 0), (0, LANES - N_AB))).astype(BF16)
    half_ab = N_DIRS * DN_HEADS
    alog = jnp.pad(dn_a_log[0].reshape(1, half_ab), ((0, 0), (0, LANES - half_ab)))
    dtb = jnp.pad(dn_dt_bias[0].reshape(1, half_ab), ((0, 0), (0, LANES - half_ab)))

    proj, gates, gates_t = _inproj(ctx, x, mod3, g_pre_mix, w_main, w_ab, alog, dtb)
    grow = gates_t.reshape(b_sz, N_DIRS * DN_HEADS, n_chunks, CHUNK)
    y_dn = _deltanet(proj, gates, grow, dn_conv[0], dn_onorm, n_ctx)

    w_gates = jnp.concatenate([_block_diag_groups(lru_w_rg[0, 0]), _block_diag_groups(lru_w_ig[0, 0]),
                               _block_diag_groups(lru_w_rg[0, 1]), _block_diag_groups(lru_w_ig[0, 1])],
                              axis=-1).astype(BF16)
    grp = lambda v: v.reshape(N_LRU_GROUPS, 1, LRU_GROUP)
    b_gates = jnp.concatenate([grp(lru_b_rg[0, 0]), grp(lru_b_ig[0, 0]),
                               grp(lru_b_rg[0, 1]), grp(lru_b_ig[0, 1])], axis=-1)
    lam = jnp.concatenate([grp(lru_lambda[0, 0]), grp(lru_lambda[0, 1])], axis=-1)
    y_lru = _rglru(proj, lru_conv[0], lru_conv_b, w_gates, b_gates, lam, n_ctx)

    h1, u2 = _merge(y_dn, y_lru, proj, x, mod3, b_merge, w_branch_dn[0].astype(BF16),
                    w_branch_lru[0].astype(BF16), w_out[0].astype(BF16), g_post_mix, g_pre_ffn, n_ctx)

    w_up_l = w_up[0]
    return _ffn(u2, h1, mod3, w_up_l[:, :D_FF].astype(BF16), w_up_l[:, D_FF:].astype(BF16),
                ffn_dw[0].reshape(9, D_FF), ffn_dw_b, w_down[0].astype(BF16), g_post_ffn)
```

```python
import functools

import jax
import jax.numpy as jnp
from jax import lax
from jax.experimental import pallas as pl
from jax.experimental.pallas import tpu as pltpu

F32 = jnp.float32
BF16 = jnp.bfloat16

EPS = 1e-6
D_MODEL = 1024
GRID_W = 64
DN_HEADS = 8
DN_HEAD_DIM = 128
DN_WIDTH = DN_HEADS * DN_HEAD_DIM
CHUNK = 64
LRU_WIDTH = 1024
LRU_BLOCKS = 16
LRU_BLOCK_DIM = LRU_WIDTH // LRU_BLOCKS
LRU_C = 8.0
N_DIRS = 2
D_FF = 4 * D_MODEL
N_AB = 2 * N_DIRS * DN_HEADS
D_MAIN = 4 * DN_WIDTH + 2 * LRU_WIDTH + 2 * D_MODEL

LANES = 128
SUBLANES = 8
BF16_ROWS = 16
LRU_GROUP = 256
N_LRU_GROUPS = LRU_WIDTH // LRU_GROUP
VMEM_LIMIT = 56 * 1024 * 1024


def _bdot(a, b):
    return jnp.dot(a.astype(BF16), b.astype(BF16), preferred_element_type=F32)


def _split(a):
    hi = a.astype(BF16)
    return hi, (a - hi.astype(F32)).astype(BF16)


def _dot3(a, b):
    a_hi, a_lo = _split(a)
    b_hi, b_lo = _split(b)
    d = functools.partial(jnp.dot, preferred_element_type=F32)
    return d(a_hi, b_hi) + (d(a_hi, b_lo) + d(a_lo, b_hi))


def _silu(x):
    return x * jax.nn.sigmoid(x)


def _softplus(x):
    return jnp.maximum(x, 0.0) + jnp.log1p(jnp.exp(-jnp.abs(x)))


def _gelu_tanh(x):
    c1 = 2.0 * 0.7978845608028654 * 1.4426950408889634
    c2 = c1 * 0.044715
    return x / (1.0 + jnp.exp2(-(x * (c1 + c2 * (x * x)))))


def _rmsnorm(v, gain):
    ms = jnp.mean(v * v, axis=-1, keepdims=True)
    return v * lax.rsqrt(ms + EPS) * gain


def _seg_conv4_rows(load, w, lo_out, n_out, t_all, n_ctx):
    lo, hi = max(lo_out - BF16_ROWS, 0), min(lo_out + n_out + BF16_ROWS, t_all)
    m = hi - lo
    x = load(lo, hi)
    touches = lambda edge: lo <= edge + 2 and hi >= edge - 2
    if touches(0) or touches(n_ctx) or touches(t_all):
        row = lax.broadcasted_iota(jnp.int32, (m, 1), 0) + lo
        in_ctx = row < n_ctx
        p = jnp.where(in_ctx, row, row - n_ctx)
        last = jnp.where(in_ctx, n_ctx, t_all - n_ctx) - 1
        tap = lambda shift, ok: jnp.where(ok, pltpu.roll(x, shift, 0), 0.0)
        taps = (tap(2, p >= 2), tap(1, p >= 1), tap(m - 1, p < last))
    else:
        taps = (pltpu.roll(x, 2, 0), pltpu.roll(x, 1, 0), pltpu.roll(x, m - 1, 0))
    acc = x * w[2:3] + taps[0] * w[0:1] + taps[1] * w[1:2] + taps[2] * w[3:4]
    return acc[lo_out - lo:lo_out - lo + n_out]


def _ada_kernel(c_ref, w_ref, b_ref, o_ref):
    o_ref[...] = _dot3(_silu(c_ref[...]), w_ref[...]) + b_ref[...]


def _ada(cc, w_ada, b_ada):
    rows, d = cc.shape
    n = w_ada.shape[1]
    tn = 1536
    return pl.pallas_call(
        _ada_kernel,
        grid=(n // tn,),
        in_specs=[pl.BlockSpec((rows, d), lambda j: (0, 0)),
                  pl.BlockSpec((d, tn), lambda j: (0, j)),
                  pl.BlockSpec((1, tn), lambda j: (0, j))],
        out_specs=pl.BlockSpec((rows, tn), lambda j: (0, j)),
        out_shape=jax.ShapeDtypeStruct((rows, n), F32),
        compiler_params=pltpu.CompilerParams(dimension_semantics=("arbitrary",),
                                             vmem_limit_bytes=VMEM_LIMIT),
        name="ada",
    )(cc, w_ada, b_ada)


def _chunk_cumsum(g, reverse):
    t_all = g.shape[0]
    pos = lax.broadcasted_iota(jnp.int32, (t_all, 1), 0) % CHUNK
    s = 1
    while s < CHUNK:
        if reverse:
            g = g + jnp.where(pos < CHUNK - s, pltpu.roll(g, t_all - s, 0), 0.0)
        else:
            g = g + jnp.where(pos >= s, pltpu.roll(g, s, 0), 0.0)
        s *= 2
    return g


def _inproj_kernel(n_ctx, ctx_ref, x_ref, shc_ref, scc_ref, shx_ref, scx_ref, g_ref, w_ref, wab_ref,
                   alog_ref, dtb_ref, o_ref, gates_ref, gates_t_ref, u_ref):
    @pl.when(pl.program_id(1) == 0)
    def _():
        gain = g_ref[...]
        u_ref[0:n_ctx] = (_rmsnorm(ctx_ref[0], gain) * (1.0 + scc_ref[0]) + shc_ref[0]).astype(BF16)
        u_ref[n_ctx:] = (_rmsnorm(x_ref[0], gain) * (1.0 + scx_ref[0]) + shx_ref[0]).astype(BF16)
        ab = jnp.dot(u_ref[...], wab_ref[...], preferred_element_type=F32)
        g = -jnp.exp(alog_ref[...]) * _softplus(ab + dtb_ref[...])
        lane = lax.broadcasted_iota(jnp.int32, (1, LANES), 1)
        gc = jnp.where(lane < DN_HEADS, _chunk_cumsum(g, False), _chunk_cumsum(g, True))
        gates = jnp.where(lane < N_DIRS * DN_HEADS, gc, jax.nn.sigmoid(ab))
        gates_ref[0] = gates
        gates_t_ref[0] = gates.T[0:N_DIRS * DN_HEADS]

    o_ref[0] = jnp.dot(u_ref[...], w_ref[...], preferred_element_type=F32).astype(BF16)


def _inproj(ctx, x, mod3, g_pre, w_main, w_ab, alog, dtb):
    b_sz, n_ctx, d = ctx.shape
    n_lat = x.shape[1]
    t_all = n_ctx + n_lat
    tn = 1024
    c_row = b_sz
    vec = lambda k, ctx_row: pl.BlockSpec(
        (1, 1, d), (lambda b, j: (c_row, 0, k)) if ctx_row else (lambda b, j: (b, 0, k)))
    return pl.pallas_call(
        functools.partial(_inproj_kernel, n_ctx),
        grid=(b_sz, D_MAIN // tn),
        in_specs=[pl.BlockSpec((1, n_ctx, d), lambda b, j: (b, 0, 0)),
                  pl.BlockSpec((1, n_lat, d), lambda b, j: (b, 0, 0)),
                  vec(0, True), vec(1, True), vec(0, False), vec(1, False),
                  pl.BlockSpec((1, d), lambda b, j: (0, 0)),
                  pl.BlockSpec((d, tn), lambda b, j: (0, j)),
                  pl.BlockSpec((d, LANES), lambda b, j: (0, 0)),
                  pl.BlockSpec((1, LANES), lambda b, j: (0, 0)),
                  pl.BlockSpec((1, LANES), lambda b, j: (0, 0))],
        out_specs=[pl.BlockSpec((1, t_all, tn), lambda b, j: (b, 0, j)),
                   pl.BlockSpec((1, t_all, LANES), lambda b, j: (b, 0, 0)),
                   pl.BlockSpec((1, N_DIRS * DN_HEADS, t_all), lambda b, j: (b, 0, 0))],
        out_shape=[jax.ShapeDtypeStruct((b_sz, t_all, D_MAIN), BF16),
                   jax.ShapeDtypeStruct((b_sz, t_all, LANES), F32),
                   jax.ShapeDtypeStruct((b_sz, N_DIRS * DN_HEADS, t_all), F32)],
        scratch_shapes=[pltpu.VMEM((t_all, d), BF16)],
        compiler_params=pltpu.CompilerParams(dimension_semantics=("arbitrary", "arbitrary"),
                                             vmem_limit_bytes=VMEM_LIMIT),
        name="inproj",
    )(ctx, x, mod3, mod3, mod3, mod3, g_pre, w_main, w_ab, alog, dtb)


DN_GROUP = 12


def _dn_group_terms(chains, q_s, k_s, v_s):
    dk = DN_HEAD_DIM
    ii = lax.broadcasted_iota(jnp.int32, (CHUNK, CHUNK), 0)
    jj = lax.broadcasted_iota(jnp.int32, (CHUNK, CHUNK), 1)
    eye = (ii == jj).astype(F32)
    nt = (((1,), (1,)), ((), ()))
    tn = (((0,), (0,)), ((), ()))

    scores = []
    for d, rows, gc, beta, gr in chains:
        k = k_s[rows, :]
        kq = jnp.concatenate([k * beta, q_s[rows, :]], axis=0).astype(BF16)
        scores.append(lax.dot_general(kq, k.astype(BF16), nt, preferred_element_type=F32))
    yield None

    low, attn = [], []
    for (d, rows, gc, beta, gr), s in zip(chains, scores):
        incl = (ii >= jj) if d == 0 else (ii <= jj)
        strict = (ii > jj) if d == 0 else (ii < jj)
        decay = jnp.where(incl, jnp.exp(jnp.minimum(gc - gr, 0.0)), 0.0)
        low.append(jnp.where(strict, s[:CHUNK] * decay, 0.0))
        attn.append((s[CHUNK:] * decay).astype(BF16))

    def joins(m):
        return ((ii // (2 * m)) == (jj // (2 * m))) & ((ii // m) != (jj // m))

    inv = [eye - jnp.where(joins(1), l, 0.0) for l in low]
    m = 2
    while m < CHUNK:
        mask = joins(m)
        half = [_bdot(t, jnp.where(mask, l, 0.0)) for t, l in zip(inv, low)]
        yield None
        inv = [t - _bdot(hf, t) for t, hf in zip(inv, half)]
        yield None
        m *= 2

    wu = []
    for (d, rows, gc, beta, gr), t in zip(chains, inv):
        kb = k_s[rows, :] * beta
        rhs = jnp.concatenate([kb * jnp.exp(gc), v_s[rows, :] * beta], axis=1)
        wu.append(_bdot(t, rhs).astype(BF16))
    yield None

    au = [jnp.dot(a, w, preferred_element_type=F32) for a, w in zip(attn, wu)]
    yield None
    ku = []
    for (d, rows, gc, beta, gr), w in zip(chains, wu):
        g_last = gc[CHUNK - 1:CHUNK] if d == 0 else gc[0:1]
        kg = (k_s[rows, :] * jnp.exp(g_last - gc)).astype(BF16)
        ku.append(lax.dot_general(kg, w, tn, preferred_element_type=F32))
    yield None

    out = []
    for (d, rows, gc, beta, gr), a, kk in zip(chains, au, ku):
        qe = q_s[rows, :] * jnp.exp(gc) - a[:, :dk]
        out.append((-kk[:, :dk], qe, kk[:, dk:], a[:, dk:]))
    yield out


def _dn_kernel(n_ctx, q_ref, k_ref, v_ref, z_ref, gates_ref, grow_ref, wq_ref, wk_ref, wv_ref, on_ref,
               o_ref, q_s, k_s, v_s, gcol_s, mq_s, add_s, o0_s, out_s):
    t_all = q_ref.shape[1]
    n_chunks = t_all // CHUNK
    ctx_chunks = n_ctx // CHUNK
    n_trips = n_chunks // DN_GROUP
    dk = DN_HEAD_DIM
    head = pl.program_id(1)
    gate_col = lambda rows, kk: gcol_s[rows, kk * DN_HEADS:kk * DN_HEADS + 1]

    def conv_silu(ref, w_ref, lo_out, n_out):
        load = lambda lo, hi: ref[0, lo:hi, :].astype(F32)
        return _silu(_seg_conv4_rows(load, w_ref[...], lo_out, n_out, t_all, n_ctx))

    def l2n(t):
        return t * lax.rsqrt(jnp.sum(t * t, axis=-1, keepdims=True) + EPS)

    def prepare(lo, n):
        rows = slice(lo, lo + n)
        q_s[rows, :] = l2n(conv_silu(q_ref, wq_ref, lo, n)) * (DN_HEAD_DIM ** -0.5)
        k_s[rows, :] = l2n(conv_silu(k_ref, wk_ref, lo, n))
        v_s[rows, :] = conv_silu(v_ref, wv_ref, lo, n)
        gcol_s[rows, :] = pltpu.roll(gates_ref[0, rows, :], (LANES - head) % LANES, 1)

    first_f = DN_GROUP * CHUNK
    first_b = (n_chunks - (DN_GROUP - ctx_chunks)) * CHUNK
    blocks = [(0, first_f // 2), (first_f // 2, first_f // 2), (first_b, t_all - first_b)]
    n_mid = 4
    mid = (first_b - first_f) // n_mid
    blocks += [(first_f + i * mid, mid) for i in range(n_mid)]
    for lo, n in blocks:
        prepare(lo, n)

    def chunk_at(d, s):
        if d == 0:
            return s
        if isinstance(s, int):
            return ctx_chunks - 1 - s if s < ctx_chunks else n_chunks + ctx_chunks - 1 - s
        return jnp.where(s < ctx_chunks, ctx_chunks - 1 - s, n_chunks + ctx_chunks - 1 - s)

    def chunk_rows(c):
        start = c * CHUNK
        return pl.ds(start if isinstance(c, int) else pl.multiple_of(start, CHUNK), CHUNK)

    def seq_step(s, states):
        new_states = []
        for d in range(N_DIRS):
            c = chunk_at(d, s)
            rows = chunk_rows(c)
            edge = c * CHUNK + (CHUNK - 1 if d == 0 else 0)
            g_last = gate_col(pl.ds(edge, 1), d)
            r = jnp.dot(mq_s[d, c], states[d].astype(BF16), preferred_element_type=F32)
            out_s[d, rows, :] = r[dk:] + o0_s[d, rows, :]
            new_states.append(states[d] * jnp.exp(g_last) + (r[:dk] + add_s[d, c]))
        return tuple(new_states)

    def trip(g_pre, g_seq, states):
        pending = [] if g_seq is None else [g_seq * DN_GROUP + j for j in range(DN_GROUP)]
        if g_pre is not None:
            chains = []
            for j in range(DN_GROUP):
                s = g_pre * DN_GROUP + j
                for d in range(N_DIRS):
                    c = chunk_at(d, s)
                    rows = chunk_rows(c)
                    chains.append((d, rows, gate_col(rows, d), gate_col(rows, N_DIRS + d),
                                   grow_ref[0, d * DN_HEADS + head, pl.ds(c, 1), :], c))
            terms = None
            for terms in _dn_group_terms([ch[:5] for ch in chains], q_s, k_s, v_s):
                if terms is None and pending:
                    states = seq_step(pending.pop(0), states)
            for (d, rows, _, _, _, c), (trans, qe, add, o0) in zip(chains, terms):
                mq_s[d, c, 0:dk, :] = trans.astype(BF16)
                mq_s[d, c, dk:, :] = qe.astype(BF16)
                add_s[d, c] = add
                o0_s[d, rows, :] = o0
        for s in pending:
            states = seq_step(s, states)
        return states

    zero = jnp.zeros((dk, dk), F32)
    states = trip(0, None, (zero, zero))
    states = lax.fori_loop(1, n_trips, lambda g, st: trip(g, g - 1, st), states)

    def finish(lo, hi):
        rows = slice(lo * CHUNK, hi * CHUNK)
        o = out_s[0, rows, :] + out_s[1, rows, :]
        y = _rmsnorm(o, on_ref[...]) * _silu(z_ref[0, rows, :].astype(F32))
        o_ref[0, lo * CHUNK - n_ctx:hi * CHUNK - n_ctx, :] = y.astype(BF16)

    done_lo = n_chunks + ctx_chunks - (n_trips - 1) * DN_GROUP
    done_hi = (n_trips - 1) * DN_GROUP
    finish(done_lo, done_hi)
    trip(None, n_trips - 1, states)
    finish(ctx_chunks, done_lo)
    finish(done_hi, n_chunks)


def _deltanet(proj, gates, grow, dn_conv, onorm, n_ctx):
    b_sz, t_all, _ = proj.shape
    n_lat = t_all - n_ctx
    n_chunks = t_all // CHUNK
    hd = DN_HEAD_DIM
    col = lambda part: pl.BlockSpec((1, t_all, hd), lambda b, h: (b, 0, part * DN_HEADS + h))
    wcol = lambda part: pl.BlockSpec((dn_conv.shape[0], hd), lambda b, h: (0, part * DN_HEADS + h))
    return pl.pallas_call(
        functools.partial(_dn_kernel, n_ctx),
        grid=(b_sz, DN_HEADS),
        in_specs=[col(0), col(1), col(2), col(3),
                  pl.BlockSpec((1, t_all, LANES), lambda b, h: (b, 0, 0)),
                  pl.BlockSpec((1, N_DIRS * DN_HEADS, n_chunks, CHUNK), lambda b, h: (b, 0, 0, 0)),
                  wcol(0), wcol(1), wcol(2),
                  pl.BlockSpec((1, hd), lambda b, h: (0, 0))],
        out_specs=pl.BlockSpec((1, n_lat, hd), lambda b, h: (b, 0, h)),
        out_shape=jax.ShapeDtypeStruct((b_sz, n_lat, DN_WIDTH), BF16),
        scratch_shapes=[pltpu.VMEM((t_all, hd), F32), pltpu.VMEM((t_all, hd), F32),
                        pltpu.VMEM((t_all, hd), F32),
                        pltpu.VMEM((t_all, LANES), F32),
                        pltpu.VMEM((N_DIRS, n_chunks, hd + CHUNK, hd), BF16),
                        pltpu.VMEM((N_DIRS, n_chunks, hd, hd), F32),
                        pltpu.VMEM((N_DIRS, t_all, hd), F32),
                        pltpu.VMEM((N_DIRS, t_all, hd), F32)],
        compiler_params=pltpu.CompilerParams(dimension_semantics=("arbitrary", "arbitrary"),
                                             vmem_limit_bytes=VMEM_LIMIT),
        name="deltanet",
    )(proj, proj, proj, proj, gates, grow, dn_conv, dn_conv, dn_conv, onorm)


LRU_ROW_TILE = 256


def _lru_kernel(n_ctx, xl_ref, yl_ref, wc_ref, bc_ref, wg_ref, bg_ref, lam_ref, o_ref,
                a_s, b_s, h_s):
    t_all = xl_ref.shape[1]
    gw = LRU_GROUP
    sp = _softplus(-lam_ref[0])
    n_half = gw // LANES
    n_sub = LRU_ROW_TILE // SUBLANES
    load_x = lambda lo, hi: xl_ref[0, lo:hi, :].astype(F32)

    def gates(i):
        r0 = i * LRU_ROW_TILE
        rows = slice(r0, r0 + LRU_ROW_TILE)
        xc = _seg_conv4_rows(load_x, wc_ref[...], r0, LRU_ROW_TILE, t_all, n_ctx) + bc_ref[...]
        y = jnp.dot(xc.astype(BF16), wg_ref[0], preferred_element_type=F32) + bg_ref[0]
        for d in range(N_DIRS):
            r = jax.nn.sigmoid(y[:, (2 * d) * gw:(2 * d + 1) * gw])
            ig = jax.nn.sigmoid(y[:, (2 * d + 1) * gw:(2 * d + 2) * gw])
            log_a = (-LRU_C) * r * sp[:, d * gw:(d + 1) * gw]
            a = jnp.exp(log_a)
            b = jnp.sqrt(-jnp.tanh(log_a) * (a * a + 1.0)) * (ig * xc)
            for hl in range(n_half):
                a_s[d * n_half + hl, rows, :] = a[:, hl * LANES:(hl + 1) * LANES]
                b_s[d * n_half + hl, rows, :] = b[:, hl * LANES:(hl + 1) * LANES]
        views = [pl.ds(r0 + k, n_sub, stride=SUBLANES) for k in range(SUBLANES)]
        for dh in range(N_DIRS * n_half):
            order = list(range(SUBLANES)) if dh < n_half else list(range(SUBLANES - 1, -1, -1))
            loaded = {k: (a_s[dh, views[k], :], b_s[dh, views[k], :]) for k in order}
            a_run, b_run = loaded[order[0]]
            scanned = {}
            for k in order[1:]:
                a_k, b_k = loaded[k]
                b_run = a_k * b_run + b_k
                a_run = a_k * a_run
                scanned[k] = (a_run, b_run)
            for k, (a_k, b_k) in scanned.items():
                a_s[dh, views[k], :] = a_k
                b_s[dh, views[k], :] = b_k

    for i in range(t_all // LRU_ROW_TILE):
        gates(i)

    n_tiles = t_all // SUBLANES
    ctx_tiles = n_ctx // SUBLANES

    def carry_step(s, carries):
        tile_of = (s, jnp.where(s < ctx_tiles, ctx_tiles - 1 - s, n_tiles + ctx_tiles - 1 - s))
        new = []
        for d in range(N_DIRS):
            rows = pl.ds(pl.multiple_of(tile_of[d] * SUBLANES, SUBLANES), SUBLANES)
            wide = lambda ref: jnp.concatenate([ref[d * n_half + hl, rows, :] for hl in range(n_half)], axis=1)
            h = wide(b_s) + wide(a_s) * carries[d]
            h_s[d, rows, :] = h
            edge = h[SUBLANES - 1:SUBLANES] if d == 0 else h[0:1]
            new.append(jnp.broadcast_to(edge, h.shape))
        return tuple(new)

    zero = jnp.zeros((SUBLANES, gw), F32)
    lax.fori_loop(0, n_tiles, carry_step, (zero, zero), unroll=8)
    h = h_s[0, n_ctx:, :] + h_s[1, n_ctx:, :]
    o_ref[0] = (h * _gelu_tanh(yl_ref[0, n_ctx:, :].astype(F32))).astype(BF16)


def _rglru(proj, lru_conv, lru_conv_b, w_gates, b_gates, lam, n_ctx):
    b_sz, t_all, _ = proj.shape
    n_lat = t_all - n_ctx
    gw = LRU_GROUP
    x_blk = 4 * DN_WIDTH // gw
    y_blk = (4 * DN_WIDTH + LRU_WIDTH) // gw
    return pl.pallas_call(
        functools.partial(_lru_kernel, n_ctx),
        grid=(b_sz, N_LRU_GROUPS),
        in_specs=[pl.BlockSpec((1, t_all, gw), lambda b, g: (b, 0, x_blk + g)),
                  pl.BlockSpec((1, t_all, gw), lambda b, g: (b, 0, y_blk + g)),
                  pl.BlockSpec((lru_conv.shape[0], gw), lambda b, g: (0, g)),
                  pl.BlockSpec((1, gw), lambda b, g: (0, g)),
                  pl.BlockSpec((1, gw, 2 * N_DIRS * gw), lambda b, g: (g, 0, 0)),
                  pl.BlockSpec((1, 1, 2 * N_DIRS * gw), lambda b, g: (g, 0, 0)),
                  pl.BlockSpec((1, 1, N_DIRS * gw), lambda b, g: (g, 0, 0))],
        out_specs=pl.BlockSpec((1, n_lat, gw), lambda b, g: (b, 0, g)),
        out_shape=jax.ShapeDtypeStruct((b_sz, n_lat, LRU_WIDTH), BF16),
        scratch_shapes=[pltpu.VMEM((N_DIRS * (gw // LANES), t_all, LANES), F32),
                        pltpu.VMEM((N_DIRS * (gw // LANES), t_all, LANES), F32),
                        pltpu.VMEM((N_DIRS, t_all, gw), F32)],
        compiler_params=pltpu.CompilerParams(dimension_semantics=("arbitrary", "arbitrary"),
                                             vmem_limit_bytes=VMEM_LIMIT),
        name="rglru",
    )(proj, proj, lru_conv, lru_conv_b, w_gates, b_gates, lam)


MERGE_ROWS = 512
MG_ROWS = 256


def _merge_kernel(ydn_ref, ylru_ref, mga_ref, mgb_ref, x_ref, bm_ref, wdn_ref, wlru_ref, wout_ref,
                  gpost_ref, gate_ref, gpre_ref, sh_ref, sc_ref, h_ref, u_ref):
    d = D_MODEL
    mg = jnp.concatenate([mga_ref[0], mgb_ref[0]], axis=0)
    gl = jax.nn.sigmoid(mg.astype(F32) + bm_ref[...])
    p_dn = jnp.dot(ydn_ref[0], wdn_ref[...], preferred_element_type=F32)
    p_lru = jnp.dot(ylru_ref[0], wlru_ref[...], preferred_element_type=F32)
    mix = _bdot(gl[:, :d] * p_dn + gl[:, d:] * p_lru, wout_ref[...])
    h = x_ref[0] + _rmsnorm(mix, gpost_ref[...]) * gate_ref[0]
    h_ref[0] = h
    u_ref[0] = (_rmsnorm(h, gpre_ref[...]) * (1.0 + sc_ref[0]) + sh_ref[0]).astype(BF16)


def _merge(y_dn, y_lru, proj, x, mod3, b_merge, w_dn, w_lru, w_out, g_post, g_pre_ffn, n_ctx):
    b_sz, n_lat, d = x.shape
    tm = MERGE_ROWS
    per = tm // MG_ROWS
    row0 = n_ctx // MG_ROWS
    mg_blk = (4 * DN_WIDTH + 2 * LRU_WIDTH) // (2 * d)
    tile = lambda: pl.BlockSpec((1, tm, d), lambda b, i: (b, i, 0))
    full = lambda r, c: pl.BlockSpec((r, c), lambda b, i: (0, 0))
    vec = lambda k: pl.BlockSpec((1, 1, d), lambda b, i: (b, 0, k))
    return pl.pallas_call(
        _merge_kernel,
        grid=(b_sz, n_lat // tm),
        in_specs=[tile(), tile(),
                  pl.BlockSpec((1, MG_ROWS, 2 * d), lambda b, i: (b, row0 + per * i, mg_blk)),
                  pl.BlockSpec((1, MG_ROWS, 2 * d), lambda b, i: (b, row0 + per * i + 1, mg_blk)),
                  tile(), full(1, 2 * d), full(d, d), full(d, d), full(d, d), full(1, d),
                  vec(2), full(1, d), vec(3), vec(4)],
        out_specs=[tile(), tile()],
        out_shape=[jax.ShapeDtypeStruct((b_sz, n_lat, d), F32),
                   jax.ShapeDtypeStruct((b_sz, n_lat, d), BF16)],
        compiler_params=pltpu.CompilerParams(dimension_semantics=("arbitrary", "arbitrary"),
                                             vmem_limit_bytes=VMEM_LIMIT),
        name="merge",
    )(y_dn, y_lru, proj, proj, x, b_merge, w_dn, w_lru, w_out, g_post, mod3, g_pre_ffn, mod3, mod3)


FFN_ROWS = 1024
FFN_TF = 512
FFN_TILES = 2
FFN_SUB = 256
FFN_HALO = SUBLANES + GRID_W


def _ffn_up(t, u_ref, wg_ref, wv_ref, g_s, val_s, r0, n_lat):
    cols = slice(t * FFN_TF, (t + 1) * FFN_TF)
    wg = wg_ref[:, cols]
    u_main = u_ref[0, pl.ds(r0, FFN_ROWS), :]
    g_s[t, FFN_HALO:FFN_HALO + FFN_ROWS, :] = jnp.dot(u_main, wg, preferred_element_type=F32)
    top0 = pl.multiple_of(jnp.maximum(r0 - GRID_W, 0), GRID_W)
    bot0 = pl.multiple_of(jnp.minimum(r0 + FFN_ROWS, n_lat - GRID_W), GRID_W)
    top = jnp.dot(u_ref[0, pl.ds(top0, GRID_W), :], wg, preferred_element_type=F32)
    bot = jnp.dot(u_ref[0, pl.ds(bot0, GRID_W), :], wg, preferred_element_type=F32)
    g_s[t, 0:SUBLANES, :] = jnp.zeros((SUBLANES, FFN_TF), F32)
    g_s[t, SUBLANES:FFN_HALO, :] = jnp.where(r0 > 0, top, 0.0)
    g_s[t, FFN_HALO + FFN_ROWS:FFN_HALO + FFN_ROWS + GRID_W, :] = jnp.where(r0 + FFN_ROWS < n_lat, bot, 0.0)
    g_s[t, FFN_HALO + FFN_ROWS + GRID_W:, :] = jnp.zeros((SUBLANES, FFN_TF), F32)
    val_s[t] = jnp.dot(u_main, wv_ref[:, cols], preferred_element_type=F32)


def _ffn_conv_act(t, dw_ref, db_ref, g_s, val_s, f_s):
    cols = slice(t * FFN_TF, (t + 1) * FFN_TF)
    dw = dw_ref[:, cols]
    bias = db_ref[:, cols]
    ext = FFN_SUB + 2 * SUBLANES
    col = lax.broadcasted_iota(jnp.int32, (FFN_SUB, 1), 0) % GRID_W
    mid = slice(SUBLANES, SUBLANES + FFN_SUB)
    for i in range(FFN_ROWS // FFN_SUB):
        base = i * FFN_SUB
        wins = [g_s[t, base + dr * GRID_W:base + dr * GRID_W + ext, :] for dr in range(3)]
        part = [sum(wins[dr] * dw[3 * dr + dc:3 * dr + dc + 1] for dr in range(3)) for dc in range(3)]
        left = jnp.where(col != 0, pltpu.roll(part[0], 1, 0)[mid], 0.0)
        right = jnp.where(col != GRID_W - 1, pltpu.roll(part[2], ext - 1, 0)[mid], 0.0)
        acc = part[1][mid] + left + right + bias
        out_rows = slice(base, base + FFN_SUB)
        f_s[t, out_rows, :] = (_gelu_tanh(acc) * val_s[t, out_rows, :]).astype(BF16)


def _ffn_kernel(u_ref, wg_ref, wv_ref, dw_ref, db_ref, wd_ref, h_ref, gate_ref, gpost_ref, o_ref,
                g_s, val_s, f_s, acc_s):
    n_lat = u_ref.shape[1]
    f_idx = pl.program_id(2)
    r0 = pl.multiple_of(pl.program_id(1) * FFN_ROWS, FFN_ROWS)

    @pl.when(f_idx == 0)
    def _():
        acc_s[...] = jnp.zeros_like(acc_s)

    for t in range(FFN_TILES):
        _ffn_up(t, u_ref, wg_ref, wv_ref, g_s, val_s, r0, n_lat)
    for t in range(FFN_TILES):
        _ffn_conv_act(t, dw_ref, db_ref, g_s, val_s, f_s)
    down = [jnp.dot(f_s[t], wd_ref[t * FFN_TF:(t + 1) * FFN_TF, :], preferred_element_type=F32)
            for t in range(FFN_TILES)]
    acc_s[...] += sum(down[1:], down[0])

    @pl.when(f_idx == pl.num_programs(2) - 1)
    def _():
        o_ref[0] = h_ref[0] + _rmsnorm(acc_s[...], gpost_ref[...]) * gate_ref[0]


def _ffn(u2, h1, mod3, w_upg, w_upv, dw9, dwb, w_down, g_post):
    b_sz, n_lat, d = h1.shape
    tf = FFN_TF * FFN_TILES
    return pl.pallas_call(
        _ffn_kernel,
        grid=(b_sz, n_lat // FFN_ROWS, D_FF // tf),
        in_specs=[pl.BlockSpec((1, n_lat, d), lambda b, r, f: (b, 0, 0)),
                  pl.BlockSpec((d, tf), lambda b, r, f: (0, f)),
                  pl.BlockSpec((d, tf), lambda b, r, f: (0, f)),
                  pl.BlockSpec((9, tf), lambda b, r, f: (0, f)),
                  pl.BlockSpec((1, tf), lambda b, r, f: (0, f)),
                  pl.BlockSpec((tf, d), lambda b, r, f: (f, 0)),
                  pl.BlockSpec((1, FFN_ROWS, d), lambda b, r, f: (b, r, 0)),
                  pl.BlockSpec((1, 1, d), lambda b, r, f: (b, 0, 5)),
                  pl.BlockSpec((1, d), lambda b, r, f: (0, 0))],
        out_specs=pl.BlockSpec((1, FFN_ROWS, d), lambda b, r, f: (b, r, 0)),
        out_shape=jax.ShapeDtypeStruct((b_sz, n_lat, d), F32),
        scratch_shapes=[pltpu.VMEM((FFN_TILES, FFN_ROWS + 2 * FFN_HALO, FFN_TF), F32),
                        pltpu.VMEM((FFN_TILES, FFN_ROWS, FFN_TF), F32),
                        pltpu.VMEM((FFN_TILES, FFN_ROWS, FFN_TF), BF16),
                        pltpu.VMEM((FFN_ROWS, d), F32)],
        compiler_params=pltpu.CompilerParams(dimension_semantics=("arbitrary", "arbitrary", "arbitrary"),
                                             vmem_limit_bytes=VMEM_LIMIT),
        name="ffn",
    )(u2, w_upg, w_upv, dw9, dwb, w_down, h1, mod3, g_post)


def _block_diag_groups(w):
    per = LRU_GROUP // LRU_BLOCK_DIM
    w4 = w.reshape(N_LRU_GROUPS, per, LRU_BLOCK_DIM, LRU_BLOCK_DIM)
    eye = jnp.eye(per, dtype=w.dtype)
    return jnp.einsum('gbij,bc->gbicj', w4, eye).reshape(N_LRU_GROUPS, LRU_GROUP, LRU_GROUP)


def kernel(x, c, ctx, c_ctx, w_ada, b_ada, g_pre_mix, g_post_mix, g_pre_ffn, g_post_ffn, w_in, b_merge, dn_conv, dn_a_log, dn_dt_bias, dn_onorm, lru_conv, lru_conv_b, lru_w_rg, lru_b_rg, lru_w_ig, lru_b_ig, lru_lambda, w_branch_dn, w_branch_lru, w_out, w_up, ffn_dw, ffn_dw_b, w_down):
    b_sz, n_lat, d = x.shape
    n_ctx = ctx.shape[1]
    t_all = n_ctx + n_lat
    n_chunks = t_all // CHUNK
    assert w_ada.shape[0] == 1, "single trunk layer"
    assert d == D_MODEL and n_ctx % MG_ROWS == 0 and MERGE_ROWS == 2 * MG_ROWS and n_lat % FFN_ROWS == 0
    assert n_ctx % CHUNK == 0 and n_lat % MERGE_ROWS == 0

    pad = (-(b_sz + 1)) % SUBLANES
    cc = jnp.concatenate([c, c_ctx[None], jnp.zeros((pad, d), F32)], axis=0)
    mod = _ada(cc, w_ada[0], b_ada)
    mod3 = mod.reshape(mod.shape[0], 1, 6 * d)

    wl = w_in[0]
    ab0 = 4 * DN_WIDTH
    w_main = jnp.concatenate([wl[:, :ab0], wl[:, ab0 + N_AB:]], axis=1).astype(BF16)
    w_ab = jnp.pad(wl[:, ab0:ab0 + N_AB], ((0, 0), (0, LANES - N_AB))).astype(BF16)
    half_ab = N_DIRS * DN_HEADS
    alog = jnp.pad(dn_a_log[0].reshape(1, half_ab), ((0, 0), (0, LANES - half_ab)))
    dtb = jnp.pad(dn_dt_bias[0].reshape(1, half_ab), ((0, 0), (0, LANES - half_ab)))

    proj, gates, gates_t = _inproj(ctx, x, mod3, g_pre_mix, w_main, w_ab, alog, dtb)
    grow = gates_t.reshape(b_sz, N_DIRS * DN_HEADS, n_chunks, CHUNK)
    y_dn = _deltanet(proj, gates, grow, dn_conv[0], dn_onorm, n_ctx)

    w_gates = jnp.concatenate([_block_diag_groups(lru_w_rg[0, 0]), _block_diag_groups(lru_w_ig[0, 0]),
                               _block_diag_groups(lru_w_rg[0, 1]), _block_diag_groups(lru_w_ig[0, 1])],
                              axis=-1).astype(BF16)
    grp = lambda v: v.reshape(N_LRU_GROUPS, 1, LRU_GROUP)
    b_gates = jnp.concatenate([grp(lru_b_rg[0, 0]), grp(lru_b_ig[0, 0]),
                               grp(lru_b_rg[0, 1]), grp(lru_b_ig[0, 1])], axis=-1)
    lam = jnp.concatenate([grp(lru_lambda[0, 0]), grp(lru_lambda[0, 1])], axis=-1)
    y_lru = _rglru(proj, lru_conv[0], lru_conv_b, w_gates, b_gates, lam, n_ctx)

    h1, u2 = _merge(y_dn, y_lru, proj, x, mod3, b_merge, w_branch_dn[0].astype(BF16),
                    w_branch_lru[0].astype(BF16), w_out[0].astype(BF16), g_post_mix, g_pre_ffn, n_ctx)

    w_up_l = w_up[0]
    return _ffn(u2, h1, mod3, w_up_l[:, :D_FF].astype(BF16), w_up_l[:, D_FF:].astype(BF16),
                ffn_dw[0].reshape(9, D_FF), ffn_dw_b, w_down[0].astype(BF16), g_post_ffn)
```

```python
import functools

import jax
import jax.numpy as jnp
from jax import lax
from jax.experimental import pallas as pl
from jax.experimental.pallas import tpu as pltpu

F32 = jnp.float32
BF16 = jnp.bfloat16

EPS = 1e-6
D_MODEL = 1024
GRID_W = 64
DN_HEADS = 8
DN_HEAD_DIM = 128
DN_WIDTH = DN_HEADS * DN_HEAD_DIM
CHUNK = 64
LRU_WIDTH = 1024
LRU_BLOCKS = 16
LRU_BLOCK_DIM = LRU_WIDTH // LRU_BLOCKS
LRU_C = 8.0
N_DIRS = 2
D_FF = 4 * D_MODEL
N_AB = 2 * N_DIRS * DN_HEADS
D_MAIN = 4 * DN_WIDTH + 2 * LRU_WIDTH + 2 * D_MODEL

LANES = 128
SUBLANES = 8
BF16_ROWS = 16
LRU_GROUP = 256
N_LRU_GROUPS = LRU_WIDTH // LRU_GROUP
VMEM_LIMIT = 56 * 1024 * 1024


def _bdot(a, b):
    return jnp.dot(a.astype(BF16), b.astype(BF16), preferred_element_type=F32)


def _split(a):
    hi = a.astype(BF16)
    return hi, (a - hi.astype(F32)).astype(BF16)


def _dot3(a, b):
    a_hi, a_lo = _split(a)
    b_hi, b_lo = _split(b)
    d = functools.partial(jnp.dot, preferred_element_type=F32)
    return d(a_hi, b_hi) + (d(a_hi, b_lo) + d(a_lo, b_hi))


def _silu(x):
    return x * jax.nn.sigmoid(x)


def _softplus(x):
    return jnp.maximum(x, 0.0) + jnp.log1p(jnp.exp(-jnp.abs(x)))


def _gelu_tanh(x):
    c1 = 2.0 * 0.7978845608028654 * 1.4426950408889634
    c2 = c1 * 0.044715
    return x / (1.0 + jnp.exp2(-(x * (c1 + c2 * (x * x)))))


def _rmsnorm(v, gain):
    ms = jnp.mean(v * v, axis=-1, keepdims=True)
    return v * lax.rsqrt(ms + EPS) * gain


def _seg_conv4_rows(load, w, lo_out, n_out, t_all, n_ctx):
    lo, hi = max(lo_out - BF16_ROWS, 0), min(lo_out + n_out + BF16_ROWS, t_all)
    m = hi - lo
    x = load(lo, hi)
    touches = lambda edge: lo <= edge + 2 and hi >= edge - 2
    if touches(0) or touches(n_ctx) or touches(t_all):
        row = lax.broadcasted_iota(jnp.int32, (m, 1), 0) + lo
        in_ctx = row < n_ctx
        p = jnp.where(in_ctx, row, row - n_ctx)
        last = jnp.where(in_ctx, n_ctx, t_all - n_ctx) - 1
        tap = lambda shift, ok: jnp.where(ok, pltpu.roll(x, shift, 0), 0.0)
        taps = (tap(2, p >= 2), tap(1, p >= 1), tap(m - 1, p < last))
    else:
        taps = (pltpu.roll(x, 2, 0), pltpu.roll(x, 1, 0), pltpu.roll(x, m - 1, 0))
    acc = x * w[2:3] + taps[0] * w[0:1] + taps[1] * w[1:2] + taps[2] * w[3:4]
    return acc[lo_out - lo:lo_out - lo + n_out]


def _ada_kernel(c_ref, w_ref, b_ref, o_ref):
    o_ref[...] = _dot3(_silu(c_ref[...]), w_ref[...]) + b_ref[...]


def _ada(cc, w_ada, b_ada):
    rows, d = cc.shape
    n = w_ada.shape[1]
    tn = 1536
    return pl.pallas_call(
        _ada_kernel,
        grid=(n // tn,),
        in_specs=[pl.BlockSpec((rows, d), lambda j: (0, 0)),
                  pl.BlockSpec((d, tn), lambda j: (0, j)),
                  pl.BlockSpec((1, tn), lambda j: (0, j))],
        out_specs=pl.BlockSpec((rows, tn), lambda j: (0, j)),
        out_shape=jax.ShapeDtypeStruct((rows, n), F32),
        compiler_params=pltpu.CompilerParams(dimension_semantics=("arbitrary",),
                                             vmem_limit_bytes=VMEM_LIMIT),
        name="ada",
    )(cc, w_ada, b_ada)


def _chunk_cumsum(g, reverse):
    t_all = g.shape[0]
    pos = lax.broadcasted_iota(jnp.int32, (t_all, 1), 0) % CHUNK
    s = 1
    while s < CHUNK:
        if reverse:
            g = g + jnp.where(pos < CHUNK - s, pltpu.roll(g, t_all - s, 0), 0.0)
        else:
            g = g + jnp.where(pos >= s, pltpu.roll(g, s, 0), 0.0)
        s *= 2
    return g


def _inproj_kernel(n_ctx, ctx_ref, x_ref, shc_ref, scc_ref, shx_ref, scx_ref, g_ref, w_ref, wab_ref,
                   alog_ref, dtb_ref, o_ref, gates_ref, gates_t_ref, u_ref):
    @pl.when(pl.program_id(1) == 0)
    def _():
        gain = g_ref[...]
        u_ref[0:n_ctx] = (_rmsnorm(ctx_ref[0], gain) * (1.0 + scc_ref[0]) + shc_ref[0]).astype(BF16)
        u_ref[n_ctx:] = (_rmsnorm(x_ref[0], gain) * (1.0 + scx_ref[0]) + shx_ref[0]).astype(BF16)
        ab = jnp.dot(u_ref[...], wab_ref[...], preferred_element_type=F32)
        g = -jnp.exp(alog_ref[...]) * _softplus(ab + dtb_ref[...])
        lane = lax.broadcasted_iota(jnp.int32, (1, LANES), 1)
        gc = jnp.where(lane < DN_HEADS, _chunk_cumsum(g, False), _chunk_cumsum(g, True))
        gates = jnp.where(lane < N_DIRS * DN_HEADS, gc, jax.nn.sigmoid(ab))
        gates_ref[0] = gates
        gates_t_ref[0] = gates.T[0:N_DIRS * DN_HEADS]

    o_ref[0] = jnp.dot(u_ref[...], w_ref[...], preferred_element_type=F32).astype(BF16)


def _inproj(ctx, x, mod3, g_pre, w_main, w_ab, alog, dtb):
    b_sz, n_ctx, d = ctx.shape
    n_lat = x.shape[1]
    t_all = n_ctx + n_lat
    tn = 1024
    c_row = b_sz
    vec = lambda k, ctx_row: pl.BlockSpec(
        (1, 1, d), (lambda b, j: (c_row, 0, k)) if ctx_row else (lambda b, j: (b, 0, k)))
    return pl.pallas_call(
        functools.partial(_inproj_kernel, n_ctx),
        grid=(b_sz, D_MAIN // tn),
        in_specs=[pl.BlockSpec((1, n_ctx, d), lambda b, j: (b, 0, 0)),
                  pl.BlockSpec((1, n_lat, d), lambda b, j: (b, 0, 0)),
                  vec(0, True), vec(1, True), vec(0, False), vec(1, False),
                  pl.BlockSpec((1, d), lambda b, j: (0, 0)),
                  pl.BlockSpec((d, tn), lambda b, j: (0, j)),
                  pl.BlockSpec((d, LANES), lambda b, j: (0, 0)),
                  pl.BlockSpec((1, LANES), lambda b, j: (0, 0)),
                  pl.BlockSpec((1, LANES), lambda b, j: (0, 0))],
        out_specs=[pl.BlockSpec((1, t_all, tn), lambda b, j: (b, 0, j)),
                   pl.BlockSpec((1, t_all, LANES), lambda b, j: (b, 0, 0)),
                   pl.BlockSpec((1, N_DIRS * DN_HEADS, t_all), lambda b, j: (b, 0, 0))],
        out_shape=[jax.ShapeDtypeStruct((b_sz, t_all, D_MAIN), BF16),
                   jax.ShapeDtypeStruct((b_sz, t_all, LANES), F32),
                   jax.ShapeDtypeStruct((b_sz, N_DIRS * DN_HEADS, t_all), F32)],
        scratch_shapes=[pltpu.VMEM((t_all, d), BF16)],
        compiler_params=pltpu.CompilerParams(dimension_semantics=("arbitrary", "arbitrary"),
                                             vmem_limit_bytes=VMEM_LIMIT),
        name="inproj",
    )(ctx, x, mod3, mod3, mod3, mod3, g_pre, w_main, w_ab, alog, dtb)


DN_GROUP = 12


def _dn_group_terms(chains, q_s, k_s, v_s):
    dk = DN_HEAD_DIM
    ii = lax.broadcasted_iota(jnp.int32, (CHUNK, CHUNK), 0)
    jj = lax.broadcasted_iota(jnp.int32, (CHUNK, CHUNK), 1)
    eye = (ii == jj).astype(F32)
    nt = (((1,), (1,)), ((), ()))
    tn = (((0,), (0,)), ((), ()))

    scores = []
    for d, rows, gc, beta, gr in chains:
        k = k_s[rows, :]
        kq = jnp.concatenate([k * beta, q_s[rows, :]], axis=0).astype(BF16)
        scores.append(lax.dot_general(kq, k.astype(BF16), nt, preferred_element_type=F32))
    yield None

    low, attn = [], []
    for (d, rows, gc, beta, gr), s in zip(chains, scores):
        incl = (ii >= jj) if d == 0 else (ii <= jj)
        strict = (ii > jj) if d == 0 else (ii < jj)
        decay = jnp.where(incl, jnp.exp(jnp.minimum(gc - gr, 0.0)), 0.0)
        low.append(jnp.where(strict, s[:CHUNK] * decay, 0.0))
        attn.append((s[CHUNK:] * decay).astype(BF16))

    def joins(m):
        return ((ii // (2 * m)) == (jj // (2 * m))) & ((ii // m) != (jj // m))

    inv = [eye - jnp.where(joins(1), l, 0.0) for l in low]
    m = 2
    while m < CHUNK:
        mask = joins(m)
        half = [_bdot(t, jnp.where(mask, l, 0.0)) for t, l in zip(inv, low)]
        yield None
        inv = [t - _bdot(hf, t) for t, hf in zip(inv, half)]
        yield None
        m *= 2

    wu = []
    for (d, rows, gc, beta, gr), t in zip(chains, inv):
        kb = k_s[rows, :] * beta
        rhs = jnp.concatenate([kb * jnp.exp(gc), v_s[rows, :] * beta], axis=1)
        wu.append(_bdot(t, rhs).astype(BF16))
    yield None

    au = [jnp.dot(a, w, preferred_element_type=F32) for a, w in zip(attn, wu)]
    yield None
    ku = []
    for (d, rows, gc, beta, gr), w in zip(chains, wu):
        g_last = gc[CHUNK - 1:CHUNK] if d == 0 else gc[0:1]
        kg = (k_s[rows, :] * jnp.exp(g_last - gc)).astype(BF16)
        ku.append(lax.dot_general(kg, w, tn, preferred_element_type=F32))
    yield None

    out = []
    for (d, rows, gc, beta, gr), a, kk in zip(chains, au, ku):
        qe = q_s[rows, :] * jnp.exp(gc) - a[:, :dk]
        out.append((-kk[:, :dk], qe, kk[:, dk:], a[:, dk:]))
    yield out


def _dn_kernel(n_ctx, q_ref, k_ref, v_ref, z_ref, gates_ref, grow_ref, wq_ref, wk_ref, wv_ref, on_ref,
               o_ref, q_s, k_s, v_s, gcol_s, mq_s, add_s, o0_s, out_s):
    t_all = q_ref.shape[1]
    n_chunks = t_all // CHUNK
    ctx_chunks = n_ctx // CHUNK
    n_trips = n_chunks // DN_GROUP
    dk = DN_HEAD_DIM
    head = pl.program_id(1)
    gate_col = lambda rows, kk: gcol_s[rows, kk * DN_HEADS:kk * DN_HEADS + 1]

    def conv_silu(ref, w_ref, lo_out, n_out):
        load = lambda lo, hi: ref[0, lo:hi, :].astype(F32)
        return _silu(_seg_conv4_rows(load, w_ref[...], lo_out, n_out, t_all, n_ctx))

    def l2n(t):
        return t * lax.rsqrt(jnp.sum(t * t, axis=-1, keepdims=True) + EPS)

    def prepare(lo, n):
        rows = slice(lo, lo + n)
        q_s[rows, :] = l2n(conv_silu(q_ref, wq_ref, lo, n)) * (DN_HEAD_DIM ** -0.5)
        k_s[rows, :] = l2n(conv_silu(k_ref, wk_ref, lo, n))
        v_s[rows, :] = conv_silu(v_ref, wv_ref, lo, n)
        gcol_s[rows, :] = pltpu.roll(gates_ref[0, rows, :], (LANES - head) % LANES, 1)

    first_f = DN_GROUP * CHUNK
    first_b = (n_chunks - (DN_GROUP - ctx_chunks)) * CHUNK
    blocks = [(0, first_f // 2), (first_f // 2, first_f // 2), (first_b, t_all - first_b)]
    n_mid = 4
    mid = (first_b - first_f) // n_mid
    blocks += [(first_f + i * mid, mid) for i in range(n_mid)]
    for lo, n in blocks:
        prepare(lo, n)

    def chunk_at(d, s):
        if d == 0:
            return s
        if isinstance(s, int):
            return ctx_chunks - 1 - s if s < ctx_chunks else n_chunks + ctx_chunks - 1 - s
        return jnp.where(s < ctx_chunks, ctx_chunks - 1 - s, n_chunks + ctx_chunks - 1 - s)

    def chunk_rows(c):
        start = c * CHUNK
        return pl.ds(start if isinstance(c, int) else pl.multiple_of(start, CHUNK), CHUNK)

    def seq_step(s, states):
        new_states = []
        for d in range(N_DIRS):
            c = chunk_at(d, s)
            rows = chunk_rows(c)
            edge = c * CHUNK + (CHUNK - 1 if d == 0 else 0)
            g_last = gate_col(pl.ds(edge, 1), d)
            r = jnp.dot(mq_s[d, c], states[d].astype(BF16), preferred_element_type=F32)
            out_s[d, rows, :] = r[dk:] + o0_s[d, rows, :]
            new_states.append(states[d] * jnp.exp(g_last) + (r[:dk] + add_s[d, c]))
        return tuple(new_states)

    def trip(g_pre, g_seq, states):
        pending = [] if g_seq is None else [g_seq * DN_GROUP + j for j in range(DN_GROUP)]
        if g_pre is not None:
            chains = []
            for j in range(DN_GROUP):
                s = g_pre * DN_GROUP + j
                for d in range(N_DIRS):
                    c = chunk_at(d, s)
                    rows = chunk_rows(c)
                    chains.append((d, rows, gate_col(rows, d), gate_col(rows, N_DIRS + d),
                                   grow_ref[0, d * DN_HEADS + head, pl.ds(c, 1), :], c))
            terms = None
            for terms in _dn_group_terms([ch[:5] for ch in chains], q_s, k_s, v_s):
                if terms is None and pending:
                    states = seq_step(pending.pop(0), states)
            for (d, rows, _, _, _, c), (trans, qe, add, o0) in zip(chains, terms):
                mq_s[d, c, 0:dk, :] = trans.astype(BF16)
                mq_s[d, c, dk:, :] = qe.astype(BF16)
                add_s[d, c] = add
                o0_s[d, rows, :] = o0
        for s in pending:
            states = seq_step(s, states)
        return states

    zero = jnp.zeros((dk, dk), F32)
    states = trip(0, None, (zero, zero))
    for g in range(1, n_trips):
        states = trip(g, g - 1, states)

    def finish(lo, hi):
        rows = slice(lo * CHUNK, hi * CHUNK)
        o = out_s[0, rows, :] + out_s[1, rows, :]
        y = _rmsnorm(o, on_ref[...]) * _silu(z_ref[0, rows, :].astype(F32))
        o_ref[0, lo * CHUNK - n_ctx:hi * CHUNK - n_ctx, :] = y.astype(BF16)

    done_lo = n_chunks + ctx_chunks - (n_trips - 1) * DN_GROUP
    done_hi = (n_trips - 1) * DN_GROUP
    finish(done_lo, done_hi)
    trip(None, n_trips - 1, states)
    finish(ctx_chunks, done_lo)
    finish(done_hi, n_chunks)


def _deltanet(proj, gates, grow, dn_conv, onorm, n_ctx):
    b_sz, t_all, _ = proj.shape
    n_lat = t_all - n_ctx
    n_chunks = t_all // CHUNK
    hd = DN_HEAD_DIM
    col = lambda part: pl.BlockSpec((1, t_all, hd), lambda b, h: (b, 0, part * DN_HEADS + h))
    wcol = lambda part: pl.BlockSpec((dn_conv.shape[0], hd), lambda b, h: (0, part * DN_HEADS + h))
    return pl.pallas_call(
        functools.partial(_dn_kernel, n_ctx),
        grid=(b_sz, DN_HEADS),
        in_specs=[col(0), col(1), col(2), col(3),
                  pl.BlockSpec((1, t_all, LANES), lambda b, h: (b, 0, 0)),
                  pl.BlockSpec((1, N_DIRS * DN_HEADS, n_chunks, CHUNK), lambda b, h: (b, 0, 0, 0)),
                  wcol(0), wcol(1), wcol(2),
                  pl.BlockSpec((1, hd), lambda b, h: (0, 0))],
        out_specs=pl.BlockSpec((1, n_lat, hd), lambda b, h: (b, 0, h)),
        out_shape=jax.ShapeDtypeStruct((b_sz, n_lat, DN_WIDTH), BF16),
        scratch_shapes=[pltpu.VMEM((t_all, hd), F32), pltpu.VMEM((t_all, hd), F32),
                        pltpu.VMEM((t_all, hd), F32),
                        pltpu.VMEM((t_all, LANES), F32),
                        pltpu.VMEM((N_DIRS, n_chunks, hd + CHUNK, hd), BF16),
                        pltpu.VMEM((N_DIRS, n_chunks, hd, hd), F32),
                        pltpu.VMEM((N_DIRS, t_all, hd), F32),
                        pltpu.VMEM((N_DIRS, t_all, hd), F32)],
        compiler_params=pltpu.CompilerParams(dimension_semantics=("arbitrary", "arbitrary"),
                                             vmem_limit_bytes=VMEM_LIMIT),
        name="deltanet",
    )(proj, proj, proj, proj, gates, grow, dn_conv, dn_conv, dn_conv, onorm)


LRU_ROW_TILE = 256


def _lru_kernel(n_ctx, xl_ref, yl_ref, wc_ref, bc_ref, wg_ref, bg_ref, lam_ref, o_ref,
                a_s, b_s, h_s):
    t_all = xl_ref.shape[1]
    gw = LRU_GROUP
    sp = _softplus(-lam_ref[0])
    n_half = gw // LANES
    n_sub = LRU_ROW_TILE // SUBLANES
    load_x = lambda lo, hi: xl_ref[0, lo:hi, :].astype(F32)

    def gates(i):
        r0 = i * LRU_ROW_TILE
        rows = slice(r0, r0 + LRU_ROW_TILE)
        xc = _seg_conv4_rows(load_x, wc_ref[...], r0, LRU_ROW_TILE, t_all, n_ctx) + bc_ref[...]
        y = jnp.dot(xc.astype(BF16), wg_ref[0], preferred_element_type=F32) + bg_ref[0]
        for d in range(N_DIRS):
            r = jax.nn.sigmoid(y[:, (2 * d) * gw:(2 * d + 1) * gw])
            ig = jax.nn.sigmoid(y[:, (2 * d + 1) * gw:(2 * d + 2) * gw])
            log_a = (-LRU_C) * r * sp[:, d * gw:(d + 1) * gw]
            a = jnp.exp(log_a)
            b = jnp.sqrt(-jnp.tanh(log_a) * (a * a + 1.0)) * (ig * xc)
            for hl in range(n_half):
                a_s[d * n_half + hl, rows, :] = a[:, hl * LANES:(hl + 1) * LANES]
                b_s[d * n_half + hl, rows, :] = b[:, hl * LANES:(hl + 1) * LANES]
        views = [pl.ds(r0 + k, n_sub, stride=SUBLANES) for k in range(SUBLANES)]
        for dh in range(N_DIRS * n_half):
            order = list(range(SUBLANES)) if dh < n_half else list(range(SUBLANES - 1, -1, -1))
            loaded = {k: (a_s[dh, views[k], :], b_s[dh, views[k], :]) for k in order}
            a_run, b_run = loaded[order[0]]
            scanned = {}
            for k in order[1:]:
                a_k, b_k = loaded[k]
                b_run = a_k * b_run + b_k
                a_run = a_k * a_run
                scanned[k] = (a_run, b_run)
            for k, (a_k, b_k) in scanned.items():
                a_s[dh, views[k], :] = a_k
                b_s[dh, views[k], :] = b_k

    for i in range(t_all // LRU_ROW_TILE):
        gates(i)

    n_tiles = t_all // SUBLANES
    ctx_tiles = n_ctx // SUBLANES

    def carry_step(s, carries):
        tile_of = (s, jnp.where(s < ctx_tiles, ctx_tiles - 1 - s, n_tiles + ctx_tiles - 1 - s))
        new = []
        for d in range(N_DIRS):
            rows = pl.ds(pl.multiple_of(tile_of[d] * SUBLANES, SUBLANES), SUBLANES)
            wide = lambda ref: jnp.concatenate([ref[d * n_half + hl, rows, :] for hl in range(n_half)], axis=1)
            h = wide(b_s) + wide(a_s) * carries[d]
            h_s[d, rows, :] = h
            edge = h[SUBLANES - 1:SUBLANES] if d == 0 else h[0:1]
            new.append(jnp.broadcast_to(edge, h.shape))
        return tuple(new)

    zero = jnp.zeros((SUBLANES, gw), F32)
    lax.fori_loop(0, n_tiles, carry_step, (zero, zero), unroll=8)
    h = h_s[0, n_ctx:, :] + h_s[1, n_ctx:, :]
    o_ref[0] = (h * _gelu_tanh(yl_ref[0, n_ctx:, :].astype(F32))).astype(BF16)


def _rglru(proj, lru_conv, lru_conv_b, w_gates, b_gates, lam, n_ctx):
    b_sz, t_all, _ = proj.shape
    n_lat = t_all - n_ctx
    gw = LRU_GROUP
    x_blk = 4 * DN_WIDTH // gw
    y_blk = (4 * DN_WIDTH + LRU_WIDTH) // gw
    return pl.pallas_call(
        functools.partial(_lru_kernel, n_ctx),
        grid=(b_sz, N_LRU_GROUPS),
        in_specs=[pl.BlockSpec((1, t_all, gw), lambda b, g: (b, 0, x_blk + g)),
                  pl.BlockSpec((1, t_all, gw), lambda b, g: (b, 0, y_blk + g)),
                  pl.BlockSpec((lru_conv.shape[0], gw), lambda b, g: (0, g)),
                  pl.BlockSpec((1, gw), lambda b, g: (0, g)),
                  pl.BlockSpec((1, gw, 2 * N_DIRS * gw), lambda b, g: (g, 0, 0)),
                  pl.BlockSpec((1, 1, 2 * N_DIRS * gw), lambda b, g: (g, 0, 0)),
                  pl.BlockSpec((1, 1, N_DIRS * gw), lambda b, g: (g, 0, 0))],
        out_specs=pl.BlockSpec((1, n_lat, gw), lambda b, g: (b, 0, g)),
        out_shape=jax.ShapeDtypeStruct((b_sz, n_lat, LRU_WIDTH), BF16),
        scratch_shapes=[pltpu.VMEM((N_DIRS * (gw // LANES), t_all, LANES), F32),
                        pltpu.VMEM((N_DIRS * (gw // LANES), t_all, LANES), F32),
                        pltpu.VMEM((N_DIRS, t_all, gw), F32)],
        compiler_params=pltpu.CompilerParams(dimension_semantics=("arbitrary", "arbitrary"),
                                             vmem_limit_bytes=VMEM_LIMIT),
        name="rglru",
    )(proj, proj, lru_conv, lru_conv_b, w_gates, b_gates, lam)


MERGE_ROWS = 512
MG_ROWS = 256


def _merge_kernel(ydn_ref, ylru_ref, mga_ref, mgb_ref, x_ref, bm_ref, wdn_ref, wlru_ref, wout_ref,
                  gpost_ref, gate_ref, gpre_ref, sh_ref, sc_ref, h_ref, u_ref):
    d = D_MODEL
    mg = jnp.concatenate([mga_ref[0], mgb_ref[0]], axis=0)
    gl = jax.nn.sigmoid(mg.astype(F32) + bm_ref[...])
    p_dn = jnp.dot(ydn_ref[0], wdn_ref[...], preferred_element_type=F32)
    p_lru = jnp.dot(ylru_ref[0], wlru_ref[...], preferred_element_type=F32)
    mix = _bdot(gl[:, :d] * p_dn + gl[:, d:] * p_lru, wout_ref[...])
    h = x_ref[0] + _rmsnorm(mix, gpost_ref[...]) * gate_ref[0]
    h_ref[0] = h
    u_ref[0] = (_rmsnorm(h, gpre_ref[...]) * (1.0 + sc_ref[0]) + sh_ref[0]).astype(BF16)


def _merge(y_dn, y_lru, proj, x, mod3, b_merge, w_dn, w_lru, w_out, g_post, g_pre_ffn, n_ctx):
    b_sz, n_lat, d = x.shape
    tm = MERGE_ROWS
    per = tm // MG_ROWS
    row0 = n_ctx // MG_ROWS
    mg_blk = (4 * DN_WIDTH + 2 * LRU_WIDTH) // (2 * d)
    tile = lambda: pl.BlockSpec((1, tm, d), lambda b, i: (b, i, 0))
    full = lambda r, c: pl.BlockSpec((r, c), lambda b, i: (0, 0))
    vec = lambda k: pl.BlockSpec((1, 1, d), lambda b, i: (b, 0, k))
    return pl.pallas_call(
        _merge_kernel,
        grid=(b_sz, n_lat // tm),
        in_specs=[tile(), tile(),
                  pl.BlockSpec((1, MG_ROWS, 2 * d), lambda b, i: (b, row0 + per * i, mg_blk)),
                  pl.BlockSpec((1, MG_ROWS, 2 * d), lambda b, i: (b, row0 + per * i + 1, mg_blk)),
                  tile(), full(1, 2 * d), full(d, d), full(d, d), full(d, d), full(1, d),
                  vec(2), full(1, d), vec(3), vec(4)],
        out_specs=[tile(), tile()],
        out_shape=[jax.ShapeDtypeStruct((b_sz, n_lat, d), F32),
                   jax.ShapeDtypeStruct((b_sz, n_lat, d), BF16)],
        compiler_params=pltpu.CompilerParams(dimension_semantics=("arbitrary", "arbitrary"),
                                             vmem_limit_bytes=VMEM_LIMIT),
        name="merge",
    )(y_dn, y_lru, proj, proj, x, b_merge, w_dn, w_lru, w_out, g_post, mod3, g_pre_ffn, mod3, mod3)


FFN_ROWS = 1024
FFN_TF = 512
FFN_TILES = 2
FFN_SUB = 256
FFN_HALO = SUBLANES + GRID_W


def _ffn_up(t, u_ref, wg_ref, wv_ref, g_s, val_s, r0, n_lat):
    cols = slice(t * FFN_TF, (t + 1) * FFN_TF)
    wg = wg_ref[:, cols]
    u_main = u_ref[0, pl.ds(r0, FFN_ROWS), :]
    g_s[t, FFN_HALO:FFN_HALO + FFN_ROWS, :] = jnp.dot(u_main, wg, preferred_element_type=F32)
    top0 = pl.multiple_of(jnp.maximum(r0 - GRID_W, 0), GRID_W)
    bot0 = pl.multiple_of(jnp.minimum(r0 + FFN_ROWS, n_lat - GRID_W), GRID_W)
    top = jnp.dot(u_ref[0, pl.ds(top0, GRID_W), :], wg, preferred_element_type=F32)
    bot = jnp.dot(u_ref[0, pl.ds(bot0, GRID_W), :], wg, preferred_element_type=F32)
    g_s[t, 0:SUBLANES, :] = jnp.zeros((SUBLANES, FFN_TF), F32)
    g_s[t, SUBLANES:FFN_HALO, :] = jnp.where(r0 > 0, top, 0.0)
    g_s[t, FFN_HALO + FFN_ROWS:FFN_HALO + FFN_ROWS + GRID_W, :] = jnp.where(r0 + FFN_ROWS < n_lat, bot, 0.0)
    g_s[t, FFN_HALO + FFN_ROWS + GRID_W:, :] = jnp.zeros((SUBLANES, FFN_TF), F32)
    val_s[t] = jnp.dot(u_main, wv_ref[:, cols], preferred_element_type=F32)


def _ffn_conv_act(t, dw_ref, db_ref, g_s, val_s, f_s):
    cols = slice(t * FFN_TF, (t + 1) * FFN_TF)
    dw = dw_ref[:, cols]
    bias = db_ref[:, cols]
    ext = FFN_SUB + 2 * SUBLANES
    col = lax.broadcasted_iota(jnp.int32, (FFN_SUB, 1), 0) % GRID_W
    mid = slice(SUBLANES, SUBLANES + FFN_SUB)
    for i in range(FFN_ROWS // FFN_SUB):
        base = i * FFN_SUB
        wins = [g_s[t, base + dr * GRID_W:base + dr * GRID_W + ext, :] for dr in range(3)]
        part = [sum(wins[dr] * dw[3 * dr + dc:3 * dr + dc + 1] for dr in range(3)) for dc in range(3)]
        left = jnp.where(col != 0, pltpu.roll(part[0], 1, 0)[mid], 0.0)
        right = jnp.where(col != GRID_W - 1, pltpu.roll(part[2], ext - 1, 0)[mid], 0.0)
        acc = part[1][mid] + left + right + bias
        out_rows = slice(base, base + FFN_SUB)
        f_s[t, out_rows, :] = (_gelu_tanh(acc) * val_s[t, out_rows, :]).astype(BF16)


def _ffn_kernel(u_ref, wg_ref, wv_ref, dw_ref, db_ref, wd_ref, h_ref, gate_ref, gpost_ref, o_ref,
                g_s, val_s, f_s, acc_s):
    n_lat = u_ref.shape[1]
    f_idx = pl.program_id(2)
    r0 = pl.multiple_of(pl.program_id(1) * FFN_ROWS, FFN_ROWS)

    @pl.when(f_idx == 0)
    def _():
        acc_s[...] = jnp.zeros_like(acc_s)

    for t in range(FFN_TILES):
        _ffn_up(t, u_ref, wg_ref, wv_ref, g_s, val_s, r0, n_lat)
    for t in range(FFN_TILES):
        _ffn_conv_act(t, dw_ref, db_ref, g_s, val_s, f_s)
    down = [jnp.dot(f_s[t], wd_ref[t * FFN_TF:(t + 1) * FFN_TF, :], preferred_element_type=F32)
            for t in range(FFN_TILES)]
    acc_s[...] += sum(down[1:], down[0])

    @pl.when(f_idx == pl.num_programs(2) - 1)
    def _():
        o_ref[0] = h_ref[0] + _rmsnorm(acc_s[...], gpost_ref[...]) * gate_ref[0]


def _ffn(u2, h1, mod3, w_upg, w_upv, dw9, dwb, w_down, g_post):
    b_sz, n_lat, d = h1.shape
    tf = FFN_TF * FFN_TILES
    return pl.pallas_call(
        _ffn_kernel,
        grid=(b_sz, n_lat // FFN_ROWS, D_FF // tf),
        in_specs=[pl.BlockSpec((1, n_lat, d), lambda b, r, f: (b, 0, 0)),
                  pl.BlockSpec((d, tf), lambda b, r, f: (0, f)),
                  pl.BlockSpec((d, tf), lambda b, r, f: (0, f)),
                  pl.BlockSpec((9, tf), lambda b, r, f: (0, f)),
                  pl.BlockSpec((1, tf), lambda b, r, f: (0, f)),
                  pl.BlockSpec((tf, d), lambda b, r, f: (f, 0)),
                  pl.BlockSpec((1, FFN_ROWS, d), lambda b, r, f: (b, r, 0)),
                  pl.BlockSpec((1, 1, d), lambda b, r, f: (b, 0, 5)),
                  pl.BlockSpec((1, d), lambda b, r, f: (0, 0))],
        out_specs=pl.BlockSpec((1, FFN_ROWS, d), lambda b, r, f: (b, r, 0)),
        out_shape=jax.ShapeDtypeStruct((b_sz, n_lat, d), F32),
        scratch_shapes=[pltpu.VMEM((FFN_TILES, FFN_ROWS + 2 * FFN_HALO, FFN_TF), F32),
                        pltpu.VMEM((FFN_TILES, FFN_ROWS, FFN_TF), F32),
                        pltpu.VMEM((FFN_TILES, FFN_ROWS, FFN_TF), BF16),
                        pltpu.VMEM((FFN_ROWS, d), F32)],
        compiler_params=pltpu.CompilerParams(dimension_semantics=("arbitrary", "arbitrary", "arbitrary"),
                                             vmem_limit_bytes=VMEM_LIMIT),
        name="ffn",
    )(u2, w_upg, w_upv, dw9, dwb, w_down, h1, mod3, g_post)


def _block_diag_groups(w):
    per = LRU_GROUP // LRU_BLOCK_DIM
    w4 = w.reshape(N_LRU_GROUPS, per, LRU_BLOCK_DIM, LRU_BLOCK_DIM)
    eye = jnp.eye(per, dtype=w.dtype)
    return jnp.einsum('gbij,bc->gbicj', w4, eye).reshape(N_LRU_GROUPS, LRU_GROUP, LRU_GROUP)


def kernel(x, c, ctx, c_ctx, w_ada, b_ada, g_pre_mix, g_post_mix, g_pre_ffn, g_post_ffn, w_in, b_merge, dn_conv, dn_a_log, dn_dt_bias, dn_onorm, lru_conv, lru_conv_b, lru_w_rg, lru_b_rg, lru_w_ig, lru_b_ig, lru_lambda, w_branch_dn, w_branch_lru, w_out, w_up, ffn_dw, ffn_dw_b, w_down):
    b_sz, n_lat, d = x.shape
    n_ctx = ctx.shape[1]
    t_all = n_ctx + n_lat
    n_chunks = t_all // CHUNK
    assert w_ada.shape[0] == 1, "single trunk layer"
    assert d == D_MODEL and n_ctx % MG_ROWS == 0 and MERGE_ROWS == 2 * MG_ROWS and n_lat % FFN_ROWS == 0
    assert n_ctx % CHUNK == 0 and n_lat % MERGE_ROWS == 0

    pad = (-(b_sz + 1)) % SUBLANES
    cc = jnp.concatenate([c, c_ctx[None], jnp.zeros((pad, d), F32)], axis=0)
    mod = _ada(cc, w_ada[0], b_ada)
    mod3 = mod.reshape(mod.shape[0], 1, 6 * d)

    wl = w_in[0]
    ab0 = 4 * DN_WIDTH
    w_main = jnp.concatenate([wl[:, :ab0], wl[:, ab0 + N_AB:]], axis=1).astype(BF16)
    w_ab = jnp.pad(wl[:, ab0:ab0 + N_AB], ((0, 0), (0, LANES - N_AB))).astype(BF16)
    half_ab = N_DIRS * DN_HEADS
    alog = jnp.pad(dn_a_log[0].reshape(1, half_ab), ((0, 0), (0, LANES - half_ab)))
    dtb = jnp.pad(dn_dt_bias[0].reshape(1, half_ab), ((0, 0), (0, LANES - half_ab)))

    proj, gates, gates_t = _inproj(ctx, x, mod3, g_pre_mix, w_main, w_ab, alog, dtb)
    grow = gates_t.reshape(b_sz, N_DIRS * DN_HEADS, n_chunks, CHUNK)
    y_dn = _deltanet(proj, gates, grow, dn_conv[0], dn_onorm, n_ctx)

    w_gates = jnp.concatenate([_block_diag_groups(lru_w_rg[0, 0]), _block_diag_groups(lru_w_ig[0, 0]),
                               _block_diag_groups(lru_w_rg[0, 1]), _block_diag_groups(lru_w_ig[0, 1])],
                              axis=-1).astype(BF16)
    grp = lambda v: v.reshape(N_LRU_GROUPS, 1, LRU_GROUP)
    b_gates = jnp.concatenate([grp(lru_b_rg[0, 0]), grp(lru_b_ig[0, 0]),
                               grp(lru_b_rg[0, 1]), grp(lru_b_ig[0, 1])], axis=-1)
    lam = jnp.concatenate([grp(lru_lambda[0, 0]), grp(lru_lambda[0, 1])], axis=-1)
    y_lru = _rglru(proj, lru_conv[0], lru_conv_b, w_gates, b_gates, lam, n_ctx)

    h1, u2 = _merge(y_dn, y_lru, proj, x, mod3, b_merge, w_branch_dn[0].astype(BF16),
                    w_branch_lru[0].astype(BF16), w_out[0].astype(BF16), g_post_mix, g_pre_ffn, n_ctx)

    w_up_l = w_up[0]
    return _ffn(u2, h1, mod3, w_up_l[:, :D_FF].astype(BF16), w_up_l[:, D_FF:].astype(BF16),
                ffn_dw[0].reshape(9, D_FF), ffn_dw_b, w_down[0].astype(BF16), g_post_ffn)
```
